```python
import jax, jax.numpy as jnp
from jax import lax
import numpy as np

D_MODEL = 2048
BATCH = 4
SEQ = 4096
DEPTH = 2
DEC_BATCH = 16
DEC_SEQ = 64
PAST_LEN = 1024

CHUNK = 64
N_A_LAYERS = DEPTH // 2
N_B_LAYERS = DEPTH - N_A_LAYERS
D_FF = ((8 * D_MODEL // 3 + 255) // 256) * 256
D_RNN = D_MODEL * 5 // 4
RNN_BLOCK = 256
N_RNN_HEADS = D_RNN // RNN_BLOCK
CONV_W = 4
LRU_C = 8.0
HEAD_DIM = 128
N_HEADS = D_MODEL // HEAD_DIM
ATTN_SCALE = HEAD_DIM ** -0.5
Q_BLOCK = 128
RMS_EPS = 1e-6
FORGET_BIAS_INIT = 3.0

kernel_name = 'hybrid_rglru_fox_yoco_step'


def rms_norm(x, g):
    xf = x.astype(jnp.float32)
    y = xf * lax.rsqrt(jnp.mean(xf * xf, axis=-1, keepdims=True) + RMS_EPS)
    return (y * g.astype(jnp.float32)).astype(x.dtype)


def swiglu_ffn(x, g, w_in, w_out):
    gate, up = jnp.split(rms_norm(x, g) @ w_in, 2, axis=-1)
    return (jax.nn.silu(gate) * up) @ w_out


def causal_conv(u, buf, w, b):
    t = u.shape[1]
    up = jnp.concatenate([buf.astype(u.dtype), u], axis=1)
    out = b
    for k in range(CONV_W):
        out = out + up[:, k:k + t] * w[k]
    return out, up[:, t:]


def rg_lru(x, h0, gate_w, gate_b, lam):
    b, t, _ = x.shape
    xf = x.astype(jnp.float32)
    xb = xf.reshape(b, t, N_RNN_HEADS, RNN_BLOCK)
    gates = jnp.einsum('bthi,ghij->gbthj', xb, gate_w.astype(jnp.float32)).reshape(2, b, t, D_RNN)
    gates = gates + gate_b.astype(jnp.float32)[:, None, None, :]
    r = jax.nn.sigmoid(gates[0])
    i = jax.nn.sigmoid(gates[1])
    log_a = -LRU_C * r * jax.nn.softplus(-lam.astype(jnp.float32))
    a = jnp.exp(log_a)
    inp = jnp.sqrt(-jnp.expm1(2.0 * log_a)) * (i * xf)
    inp = inp.at[:, 0].add(a[:, 0] * h0.astype(jnp.float32))

    def combine(c1, c2):
        a1, b1 = c1
        a2, b2 = c2
        return a1 * a2, a2 * b1 + b2

    _, h = lax.associative_scan(combine, (a, inp), axis=1)
    return h.astype(x.dtype), h[:, -1].astype(x.dtype)


def rglru_layer(x, conv_buf, h0, norm, w_in, conv_w, conv_b, gate_w, gate_b, lam, w_out):
    xn = rms_norm(x, norm)
    gate, u = jnp.split(xn @ w_in, 2, axis=-1)
    u, new_buf = causal_conv(u, conv_buf, conv_w, conv_b)
    h, h_last = rg_lru(u, h0, gate_w, gate_b, lam)
    return (h * jax.nn.gelu(gate)) @ w_out, new_buf, h_last


def shared_kv(x, kv_norm, w_kv, w_f, b_f):
    b, t = x.shape[:2]
    xn = rms_norm(x, kv_norm)
    k, v = jnp.split(xn @ w_kv, 2, axis=-1)
    logf = jax.nn.log_sigmoid((xn @ w_f).astype(jnp.float32) + b_f.astype(jnp.float32))
    return (k.reshape(b, t, N_HEADS, HEAD_DIM), v.reshape(b, t, N_HEADS, HEAD_DIM), logf)


def fox_attend(q, k, v, cq, ck, q_pos, k_pos):
    s = jnp.einsum('bqhd,bkhd->bhqk', q, k).astype(jnp.float32) * ATTN_SCALE
    s = s + jnp.swapaxes(cq, 1, 2)[:, :, :, None] - jnp.swapaxes(ck, 1, 2)[:, :, None, :]
    s = jnp.where(k_pos[None, None, None, :] <= q_pos[None, None, :, None], s, -jnp.inf)
    p = jax.nn.softmax(s, axis=-1)
    return jnp.einsum('bhqk,bkhd->bqhd', p.astype(v.dtype), v)


def fox_prompt(q, k, v, logf):
    b, s, h, d = q.shape
    c = jnp.cumsum(logf, axis=1)
    nb = s // Q_BLOCK
    qb = jnp.swapaxes(q.reshape(b, nb, Q_BLOCK, h, d), 0, 1)
    cqb = jnp.swapaxes(c.reshape(b, nb, Q_BLOCK, h), 0, 1)
    k_pos = jnp.arange(s)

    def one_block(args):
        qi, cqi, bi = args
        q_pos = bi * Q_BLOCK + jnp.arange(Q_BLOCK)
        return fox_attend(qi, k, v, cqi, c, q_pos, k_pos)

    o = lax.map(one_block, (qb, cqb, jnp.arange(nb)))
    return jnp.swapaxes(o, 0, 1).reshape(b, s, h, d)


def fox_sample(q, k_new, v_new, logf_new, cache_k, cache_v, cache_logf):
    past = cache_k.shape[1]
    t = q.shape[1]
    k = jnp.concatenate([cache_k, k_new], axis=1)
    v = jnp.concatenate([cache_v, v_new], axis=1)
    c = jnp.cumsum(jnp.concatenate([cache_logf.astype(jnp.float32), logf_new], axis=1), axis=1)
    return fox_attend(q, k, v, c[:, past:], c, past + jnp.arange(t), jnp.arange(past + t))


def fox_query(x, norm, w_qg):
    b, t = x.shape[:2]
    q, g = jnp.split(rms_norm(x, norm) @ w_qg, 2, axis=-1)
    return q.reshape(b, t, N_HEADS, HEAD_DIM), g


def fox_output(o, g, w_o):
    b, t = o.shape[:2]
    return (o.reshape(b, t, N_HEADS * HEAD_DIM) * jax.nn.sigmoid(g)) @ w_o


def setup_inputs(seed: int = 0) -> dict:
    key = jax.random.key(seed)
    ks = jax.random.split(key, 28)
    f32 = jnp.float32

    def nrm(k, shape, scale=1.0):
        return jax.random.normal(k, shape, f32) * scale

    a0 = jax.random.uniform(ks[14], (N_A_LAYERS, D_RNN), f32, 0.9, 0.999)
    p = a0 ** (1.0 / LRU_C)
    return {
        'x_prompt': nrm(ks[0], (BATCH, SEQ, D_MODEL)),
        'x_sample': nrm(ks[1], (DEC_BATCH, DEC_SEQ, D_MODEL)),
        'state_conv': nrm(ks[2], (N_A_LAYERS, DEC_BATCH, CONV_W - 1, D_RNN)),
        'state_h': nrm(ks[3], (N_A_LAYERS, DEC_BATCH, D_RNN), 0.5),
        'cache_k': nrm(ks[4], (DEC_BATCH, PAST_LEN, N_HEADS, HEAD_DIM)),
        'cache_v': nrm(ks[5], (DEC_BATCH, PAST_LEN, N_HEADS, HEAD_DIM)),
        'cache_logf': jax.nn.log_sigmoid(FORGET_BIAS_INIT + nrm(ks[6], (DEC_BATCH, PAST_LEN, N_HEADS), 0.5)),
        'ffn_norm': 1.0 + nrm(ks[7], (DEPTH, 2, D_MODEL), 0.05),
        'ffn_w_in': nrm(ks[8], (DEPTH, 2, D_MODEL, 2 * D_FF), D_MODEL ** -0.5),
        'ffn_w_out': nrm(ks[9], (DEPTH, 2, D_FF, D_MODEL), D_FF ** -0.5),
        'a_norm': 1.0 + nrm(ks[10], (N_A_LAYERS, D_MODEL), 0.05),
        'a_w_in': nrm(ks[11], (N_A_LAYERS, D_MODEL, 2 * D_RNN), D_MODEL ** -0.5),
        'a_conv_w': nrm(ks[12], (N_A_LAYERS, CONV_W, D_RNN), CONV_W ** -0.5),
        'a_conv_b': nrm(ks[13], (N_A_LAYERS, D_RNN), 0.01),
        'a_gate_w': nrm(ks[15], (N_A_LAYERS, 2, N_RNN_HEADS, RNN_BLOCK, RNN_BLOCK), RNN_BLOCK ** -0.5),
        'a_gate_b': nrm(ks[16], (N_A_LAYERS, 2, D_RNN), 0.01),
        'a_lambda': jnp.log(p) - jnp.log1p(-p),
        'a_w_out': nrm(ks[17], (N_A_LAYERS, D_RNN, D_MODEL), D_RNN ** -0.5),
        'kv_norm': 1.0 + nrm(ks[18], (D_MODEL,), 0.05),
        'w_kv': nrm(ks[19], (D_MODEL, 2 * D_MODEL), D_MODEL ** -0.5),
        'w_f': nrm(ks[20], (D_MODEL, N_HEADS), 0.5 * D_MODEL ** -0.5),
        'b_f': FORGET_BIAS_INIT + nrm(ks[21], (N_HEADS,), 0.1),
        'b_norm': 1.0 + nrm(ks[22], (N_B_LAYERS, D_MODEL), 0.05),
        'b_w_qg': nrm(ks[23], (N_B_LAYERS, D_MODEL, 2 * D_MODEL), D_MODEL ** -0.5),
        'b_w_o': nrm(ks[24], (N_B_LAYERS, D_MODEL, D_MODEL), D_MODEL ** -0.5),
        'final_norm': 1.0 + nrm(ks[25], (D_MODEL,), 0.05),
    }


def reference(x_prompt, x_sample, state_conv, state_h, cache_k, cache_v, cache_logf,
              ffn_norm, ffn_w_in, ffn_w_out, a_norm, a_w_in, a_conv_w, a_conv_b, a_gate_w,
              a_gate_b, a_lambda, a_w_out, kv_norm, w_kv, w_f, b_f, b_norm, b_w_qg, b_w_o,
              final_norm):
    xp, xs = x_prompt, x_sample
    bp = x_prompt.shape[0]
    conv_p, h_p, conv_s, h_s = [], [], [], []
    for l in range(DEPTH):
        xp = xp + 0.5 * swiglu_ffn(xp, ffn_norm[l, 0], ffn_w_in[l, 0], ffn_w_out[l, 0])
        xs = xs + 0.5 * swiglu_ffn(xs, ffn_norm[l, 0], ffn_w_in[l, 0], ffn_w_out[l, 0])
        if l < N_A_LAYERS:
            buf0 = jnp.zeros((bp, CONV_W - 1, D_RNN), xp.dtype)
            h0 = jnp.zeros((bp, D_RNN), xp.dtype)
            yp, cp, hp = rglru_layer(xp, buf0, h0, a_norm[l], a_w_in[l], a_conv_w[l], a_conv_b[l],
                                     a_gate_w[l], a_gate_b[l], a_lambda[l], a_w_out[l])
            ys, cs, hs = rglru_layer(xs, state_conv[l], state_h[l], a_norm[l], a_w_in[l], a_conv_w[l],
                                     a_conv_b[l], a_gate_w[l], a_gate_b[l], a_lambda[l], a_w_out[l])
            conv_p.append(cp)
            h_p.append(hp)
            conv_s.append(cs)
            h_s.append(hs)
        else:
            j = l - N_A_LAYERS
            qp, gp = fox_query(xp, b_norm[j], b_w_qg[j])
            yp = fox_output(fox_prompt(qp, kp, vp, fp), gp, b_w_o[j])
            qs, gs = fox_query(xs, b_norm[j], b_w_qg[j])
            ys = fox_output(fox_sample(qs, ks_, vs_, fs_, cache_k, cache_v, cache_logf), gs, b_w_o[j])
        xp = xp + yp
        xs = xs + ys
        xp = xp + 0.5 * swiglu_ffn(xp, ffn_norm[l, 1], ffn_w_in[l, 1], ffn_w_out[l, 1])
        xs = xs + 0.5 * swiglu_ffn(xs, ffn_norm[l, 1], ffn_w_in[l, 1], ffn_w_out[l, 1])
        if l == N_A_LAYERS - 1:
            kp, vp, fp = shared_kv(xp, kv_norm, w_kv, w_f, b_f)
            ks_, vs_, fs_ = shared_kv(xs, kv_norm, w_kv, w_f, b_f)
    y_prompt = rms_norm(xp, final_norm)
    y_sample = rms_norm(xs, final_norm)
    return (y_prompt, y_sample, jnp.stack(conv_p), jnp.stack(h_p), kp, vp, fp,
            jnp.stack(conv_s), jnp.stack(h_s), ks_, vs_, fs_)
```

```python
import functools

import jax
import jax.numpy as jnp
from jax import lax
from jax.experimental import pallas as pl
from jax.experimental.pallas import tpu as pltpu

F32 = jnp.float32
BF16 = jnp.bfloat16

RMS_EPS = 1e-6
LRU_C = 8.0
HEAD_DIM = 128
ATTN_SCALE = HEAD_DIM ** -0.5
MASK_VALUE = -1e30

V7X_LANES = 128
V7X_SUBLANES = 8
V7X_VMEM_BYTES = 64 * 1024 * 1024
VMEM_LIMIT_BYTES = V7X_VMEM_BYTES - 8 * 1024 * 1024

RNN_BLOCK = 256
CONV_W = 4
CONV_PAD_ROWS = V7X_SUBLANES

AUG_Q_LANES = (0, 1, 2)
AUG_K_LANES = (3, 4, 5)


def _params(*semantics):
    return pltpu.CompilerParams(dimension_semantics=semantics, vmem_limit_bytes=VMEM_LIMIT_BYTES)


def _rms(x, g):
    y = x * lax.rsqrt(jnp.mean(x * x, axis=-1, keepdims=True) + RMS_EPS)
    return y * g


def _ffn_kernel(x_ref, g_ref, wg_ref, wu_ref, wo_ref, fin_ref, o_ref, xn_ref, *, apply_final):
    j = pl.program_id(1)

    @pl.when(j == 0)
    def _():
        x = x_ref[...]
        xn_ref[...] = _rms(x, g_ref[...]).astype(BF16)
        o_ref[...] = x

    xn = xn_ref[...]
    gate = jnp.dot(xn, wg_ref[...], preferred_element_type=F32)
    up = jnp.dot(xn, wu_ref[...], preferred_element_type=F32)
    h = (0.5 * (gate * jax.nn.sigmoid(gate)) * up).astype(BF16)
    o_ref[...] += jnp.dot(h, wo_ref[...], preferred_element_type=F32)

    if apply_final:
        @pl.when(j == pl.num_programs(1) - 1)
        def _():
            o_ref[...] = _rms(o_ref[...], fin_ref[...])


def _ffn(x, g, w_in, w_out, fin, *, apply_final, tm, tf):
    n, d = x.shape
    dff = w_out.shape[0]
    nj = dff // tf
    assert n % tm == 0 and dff % tf == 0 and w_in.shape == (d, 2 * dff)
    return pl.pallas_call(
        functools.partial(_ffn_kernel, apply_final=apply_final),
        grid=(n // tm, nj),
        in_specs=[
            pl.BlockSpec((tm, d), lambda i, j: (i, 0)),
            pl.BlockSpec((1, d), lambda i, j: (0, 0)),
            pl.BlockSpec((d, tf), lambda i, j: (0, j)),
            pl.BlockSpec((d, tf), lambda i, j: (0, j + nj)),
            pl.BlockSpec((tf, d), lambda i, j: (j, 0)),
            pl.BlockSpec((1, d), lambda i, j: (0, 0)),
        ],
        out_specs=pl.BlockSpec((tm, d), lambda i, j: (i, 0)),
        out_shape=jax.ShapeDtypeStruct((n, d), F32),
        scratch_shapes=[pltpu.VMEM((tm, d), BF16)],
        compiler_params=_params("arbitrary", "arbitrary"),
        name="ffn",
    )(x, g.reshape(1, d), w_in, w_in, w_out, fin.reshape(1, d))


def _proj2_kernel(x_ref, g_ref, wa_ref, wb_ref, oa_ref, ob_ref, xn_ref):
    @pl.when(pl.program_id(1) == 0)
    def _():
        xn_ref[...] = _rms(x_ref[...], g_ref[...]).astype(BF16)

    xn = xn_ref[...]
    oa_ref[...] = jnp.dot(xn, wa_ref[...], preferred_element_type=F32)
    ob_ref[...] = jnp.dot(xn, wb_ref[...], preferred_element_type=F32)


def _proj2(x, g, w, *, tm, tn):
    n, d = x.shape
    half = w.shape[1] // 2
    nj = half // tn
    assert n % tm == 0 and half % tn == 0
    out = jax.ShapeDtypeStruct((n, half), F32)
    return pl.pallas_call(
        _proj2_kernel,
        grid=(n // tm, nj),
        in_specs=[
            pl.BlockSpec((tm, d), lambda i, j: (i, 0)),
            pl.BlockSpec((1, d), lambda i, j: (0, 0)),
            pl.BlockSpec((d, tn), lambda i, j: (0, j)),
            pl.BlockSpec((d, tn), lambda i, j: (0, j + nj)),
        ],
        out_specs=[pl.BlockSpec((tm, tn), lambda i, j: (i, j))] * 2,
        out_shape=[out, out],
        scratch_shapes=[pltpu.VMEM((tm, d), BF16)],
        compiler_params=_params("arbitrary", "arbitrary"),
        name="proj2",
    )(x, g.reshape(1, d), w, w)


def _log_sigmoid(z):
    return jnp.minimum(z, 0.0) - jnp.log1p(jnp.exp(-jnp.abs(z)))


def _kv_kernel(x_ref, g_ref, wk_ref, wv_ref, wf_ref, bf_ref, k_ref, v_ref, f_ref, xn_ref):
    @pl.when(pl.program_id(1) == 0)
    def _():
        xn = _rms(x_ref[...], g_ref[...]).astype(BF16)
        xn_ref[...] = xn
        z = jnp.dot(xn, wf_ref[...], preferred_element_type=F32) + bf_ref[...]
        f_ref[...] = _log_sigmoid(z)

    xn = xn_ref[...]
    k_ref[...] = jnp.dot(xn, wk_ref[...], preferred_element_type=F32)
    v_ref[...] = jnp.dot(xn, wv_ref[...], preferred_element_type=F32)


def _kv_proj(x, g, w_kv, wf_pad, bf_pad, *, tm, tn):
    n, d = x.shape
    half = w_kv.shape[1] // 2
    nj = half // tn
    assert n % tm == 0 and half % tn == 0
    out = jax.ShapeDtypeStruct((n, half), F32)
    return pl.pallas_call(
        _kv_kernel,
        grid=(n // tm, nj),
        in_specs=[
            pl.BlockSpec((tm, d), lambda i, j: (i, 0)),
            pl.BlockSpec((1, d), lambda i, j: (0, 0)),
            pl.BlockSpec((d, tn), lambda i, j: (0, j)),
            pl.BlockSpec((d, tn), lambda i, j: (0, j + nj)),
            pl.BlockSpec((d, V7X_LANES), lambda i, j: (0, 0)),
            pl.BlockSpec((1, V7X_LANES), lambda i, j: (0, 0)),
        ],
        out_specs=[
            pl.BlockSpec((tm, tn), lambda i, j: (i, j)),
            pl.BlockSpec((tm, tn), lambda i, j: (i, j)),
            pl.BlockSpec((tm, V7X_LANES), lambda i, j: (i, 0)),
        ],
        out_shape=[out, out, jax.ShapeDtypeStruct((n, V7X_LANES), F32)],
        scratch_shapes=[pltpu.VMEM((tm, d), BF16)],
        compiler_params=_params("arbitrary", "arbitrary"),
        name="kv_proj",
    )(x, g.reshape(1, d), w_kv, w_kv, wf_pad, bf_pad)


def _out_proj_kernel(a_ref, w_ref, x_ref, o_ref):
    o_ref[...] = x_ref[...] + jnp.dot(a_ref[...], w_ref[...], preferred_element_type=F32)


def _out_proj(a, w, x, *, tm):
    n, k = a.shape
    d = w.shape[1]
    assert n % tm == 0
    return pl.pallas_call(
        _out_proj_kernel,
        grid=(n // tm,),
        in_specs=[
            pl.BlockSpec((tm, k), lambda i: (i, 0)),
            pl.BlockSpec((k, d), lambda i: (0, 0)),
            pl.BlockSpec((tm, d), lambda i: (i, 0)),
        ],
        out_specs=pl.BlockSpec((tm, d), lambda i: (i, 0)),
        out_shape=jax.ShapeDtypeStruct((n, d), F32),
        compiler_params=_params("arbitrary"),
        name="out_proj",
    )(a, w, x)


def _gelu_tanh(x):
    return 0.5 * x * (1.0 + jnp.tanh(0.7978845608028654 * (x + 0.044715 * (x * x * x))))


def _softplus(y):
    return jnp.maximum(y, 0.0) + jnp.log1p(jnp.exp(-jnp.abs(y)))


def _scan_rows(a_ref, b_ref, cols, h_in, tt):
    width = cols.stop - cols.start
    row = lax.broadcasted_iota(jnp.int32, (V7X_SUBLANES, width), 0)

    def group(gi, hc):
        r0 = pl.multiple_of(gi * V7X_SUBLANES, V7X_SUBLANES)
        a = a_ref[pl.ds(r0, V7X_SUBLANES), cols]
        b = b_ref[pl.ds(r0, V7X_SUBLANES), cols]
        for s in (1, 2, 4):
            keep = row >= s
            a_sh = pltpu.roll(a, s, 0)
            b_sh = pltpu.roll(b, s, 0)
            b = jnp.where(keep, a * b_sh + b, b)
            a = jnp.where(keep, a * a_sh, a)
        h = a * hc + b
        b_ref[pl.ds(r0, V7X_SUBLANES), cols] = h
        return h[V7X_SUBLANES - 1:V7X_SUBLANES, :]

    return lax.fori_loop(0, tt // V7X_SUBLANES, group, h_in, unroll=2)


def _rglru_kernel(gate_ref, u_ref, conv0_ref, h0_ref, cw_ref, cb_ref, gw_ref, gb_ref, lam_ref,
                  hg_ref, nconv_ref, nh_ref, ubuf, a_s, b_s, hcar, *, tt):
    c = u_ref.shape[-1]
    pad = CONV_PAD_ROWS

    @pl.when(pl.program_id(1) == 0)
    def _():
        ubuf[0:pad, :] = conv0_ref[0]
        hcar[...] = h0_ref[0]

    ubuf[pad:pad + tt, :] = u_ref[0]

    for blk in range(c // RNN_BLOCK):
        cols = slice(blk * RNN_BLOCK, (blk + 1) * RNN_BLOCK)
        uc = cb_ref[:, cols]
        for k in range(CONV_W):
            uc = uc + ubuf[pad - (CONV_W - 1 - k):pad - (CONV_W - 1 - k) + tt, cols] * cw_ref[k:k + 1, cols]
        ub = uc.astype(BF16)
        r = jax.nn.sigmoid(jnp.dot(ub, gw_ref[0, blk], preferred_element_type=F32) + gb_ref[0:1, cols])
        i = jax.nn.sigmoid(jnp.dot(ub, gw_ref[1, blk], preferred_element_type=F32) + gb_ref[1:2, cols])
        log_a = (-LRU_C * r) * _softplus(-lam_ref[:, cols])
        a = jnp.exp(log_a)
        a_s[:, cols] = a
        b_s[:, cols] = jnp.sqrt(1.0 - a * a) * (i * uc)
        h_last = _scan_rows(a_s, b_s, cols, hcar[:, cols], tt)
        hcar[:, cols] = h_last
        hg_ref[0, :, cols] = (b_s[:, cols] * _gelu_tanh(gate_ref[0, :, cols])).astype(BF16)

    tail = ubuf[pad + tt - (CONV_W - 1):pad + tt, :]
    nconv_ref[0] = tail
    ubuf[pad - (CONV_W - 1):pad, :] = tail
    nh_ref[0] = hcar[...]


def _rglru(gate, u, conv0, h0, cw, cb, gw, gb, lam, *, tt):
    b, t, c = u.shape
    assert t % tt == 0 and tt % V7X_SUBLANES == 0 and tt >= CONV_W - 1
    nblk = c // RNN_BLOCK
    conv0_pad = jnp.pad(conv0, ((0, 0), (CONV_PAD_ROWS - (CONV_W - 1), 0), (0, 0)))
    return pl.pallas_call(
        functools.partial(_rglru_kernel, tt=tt),
        grid=(b, t // tt),
        in_specs=[
            pl.BlockSpec((1, tt, c), lambda bi, ti: (bi, ti, 0)),
            pl.BlockSpec((1, tt, c), lambda bi, ti: (bi, ti, 0)),
            pl.BlockSpec((1, CONV_PAD_ROWS, c), lambda bi, ti: (bi, 0, 0)),
            pl.BlockSpec((1, 1, c), lambda bi, ti: (bi, 0, 0)),
            pl.BlockSpec((CONV_W, c), lambda bi, ti: (0, 0)),
            pl.BlockSpec((1, c), lambda bi, ti: (0, 0)),
            pl.BlockSpec((2, nblk, RNN_BLOCK, RNN_BLOCK), lambda bi, ti: (0, 0, 0, 0)),
            pl.BlockSpec((2, c), lambda bi, ti: (0, 0)),
            pl.BlockSpec((1, c), lambda bi, ti: (0, 0)),
        ],
        out_specs=[
            pl.BlockSpec((1, tt, c), lambda bi, ti: (bi, ti, 0)),
            pl.BlockSpec((1, CONV_W - 1, c), lambda bi, ti: (bi, 0, 0)),
            pl.BlockSpec((1, 1, c), lambda bi, ti: (bi, 0, 0)),
        ],
        out_shape=[
            jax.ShapeDtypeStruct((b, t, c), BF16),
            jax.ShapeDtypeStruct((b, CONV_W - 1, c), F32),
            jax.ShapeDtypeStruct((b, 1, c), F32),
        ],
        scratch_shapes=[
            pltpu.VMEM((CONV_PAD_ROWS + tt, c), F32),
            pltpu.VMEM((tt, c), F32),
            pltpu.VMEM((tt, c), F32),
            pltpu.VMEM((1, c), F32),
        ],
        compiler_params=_params("arbitrary", "arbitrary"),
        name="rglru",
    )(gate, u, conv0_pad, h0.reshape(b, 1, c), cw, cb.reshape(1, c), gw, gb, lam.reshape(1, c))


def _cumsum_kernel(x_ref, o_ref, *, rows):
    s = x_ref.shape[1]
    r = lax.broadcasted_iota(jnp.int32, (rows, rows), 0)
    q = lax.broadcasted_iota(jnp.int32, (rows, rows), 1)
    tril = (q <= r).astype(F32)

    def block(bi, carry):
        r0 = pl.multiple_of(bi * rows, rows)
        cs = jnp.dot(tril, x_ref[0, pl.ds(r0, rows), :], preferred_element_type=F32,
                     precision=lax.Precision.HIGHEST) + carry
        o_ref[0, pl.ds(r0, rows), :] = cs
        return cs[rows - 1:rows, :]

    lax.fori_loop(0, s // rows, block, jnp.zeros((1, x_ref.shape[2]), F32))


def _cumsum(x, *, rows):
    b, s, w = x.shape
    assert s % rows == 0
    return pl.pallas_call(
        functools.partial(_cumsum_kernel, rows=rows),
        grid=(b,),
        in_specs=[pl.BlockSpec((1, s, w), lambda bi: (bi, 0, 0))],
        out_specs=pl.BlockSpec((1, s, w), lambda bi: (bi, 0, 0)),
        out_shape=jax.ShapeDtypeStruct((b, s, w), F32),
        compiler_params=_params("arbitrary"),
        name="cumsum",
    )(x)


def _split3(c):
    hi = c.astype(BF16)
    r1 = c - hi.astype(F32)
    mid = r1.astype(BF16)
    lo = (r1 - mid.astype(F32)).astype(BF16)
    return hi, mid, lo


def _aug_lanes(c, head, piece_lanes, one_lanes, sign):
    w = c.shape[1]
    src = lax.broadcasted_iota(jnp.int32, (w, V7X_LANES), 0)
    dst = lax.broadcasted_iota(jnp.int32, (w, V7X_LANES), 1)
    out = None
    for piece, lane in zip(_split3(c), piece_lanes):
        route = jnp.where((src == head) & (dst == lane), sign, 0.0).astype(BF16)
        term = jnp.dot(piece, route, preferred_element_type=F32)
        out = term if out is None else out + term
    lane_id = lax.broadcasted_iota(jnp.int32, out.shape, 1)
    ones = (lane_id >= one_lanes[0]) & (lane_id <= one_lanes[-1])
    return jnp.where(ones, 1.0, out)


def _attn_prompt_kernel(q_ref, k_ref, v_ref, g_ref, c_ref, o_ref,
                        qa_ref, ka_ref, vb_ref, m_ref, l_ref, acc_ref, *, tq, build_rows):
    head = pl.program_id(1)
    qi = pl.program_id(2)
    s_len = k_ref.shape[1]
    reps = tq // V7X_LANES

    @pl.when(qi == 0)
    def _():
        def build(bi, carry):
            r0 = pl.multiple_of(bi * build_rows, build_rows)
            rows = pl.ds(r0, build_rows)
            ka_ref[rows, 0:HEAD_DIM] = k_ref[0, rows, :].astype(BF16)
            ka_ref[rows, HEAD_DIM:] = _aug_lanes(c_ref[0, rows, :], head, AUG_K_LANES, AUG_Q_LANES,
                                                 -1.0).astype(BF16)
            vb_ref[rows, :] = v_ref[0, rows, :].astype(BF16)
            return carry
        lax.fori_loop(0, s_len // build_rows, build, 0)

    q0 = pl.multiple_of(qi * tq, tq)
    qa_ref[:, 0:HEAD_DIM] = (q_ref[0] * ATTN_SCALE).astype(BF16)
    qa_ref[:, HEAD_DIM:] = _aug_lanes(c_ref[0, pl.ds(q0, tq), :], head, AUG_Q_LANES, AUG_K_LANES,
                                      1.0).astype(BF16)
    m_ref[...] = jnp.full(m_ref.shape, MASK_VALUE, F32)
    l_ref[...] = jnp.zeros(l_ref.shape, F32)
    acc_ref[...] = jnp.zeros(acc_ref.shape, F32)

    def tile(kj, masked):
        k0 = pl.multiple_of(kj * tq, tq)
        s = lax.dot_general(qa_ref[...], ka_ref[pl.ds(k0, tq), :], (((1,), (1,)), ((), ())),
                            preferred_element_type=F32)
        if masked:
            row = lax.broadcasted_iota(jnp.int32, (tq, tq), 0)
            col = lax.broadcasted_iota(jnp.int32, (tq, tq), 1)
            s = jnp.where(col <= row, s, MASK_VALUE)
        m_prev = m_ref[...]
        m_new = jnp.maximum(m_prev, jnp.max(s, axis=1, keepdims=True))
        alpha = jnp.exp(m_prev - m_new)
        p = jnp.exp(s - jnp.tile(m_new, (1, reps)))
        l_ref[...] = alpha * l_ref[...] + jnp.sum(p, axis=1, keepdims=True)
        acc_ref[...] = alpha * acc_ref[...] + jnp.dot(p.astype(BF16), vb_ref[pl.ds(k0, tq), :],
                                                      preferred_element_type=F32)
        m_ref[...] = m_new

    def full_tile(kj, carry):
        tile(kj, False)
        return carry

    lax.fori_loop(0, qi, full_tile, 0)
    tile(qi, True)
    o = acc_ref[...] / l_ref[...]
    o_ref[0] = (o * jax.nn.sigmoid(g_ref[0])).astype(BF16)


def _attn_prompt(q, k, v, g, c, *, tq, build_rows):
    b, s, d = q.shape
    nh = d // HEAD_DIM
    assert s % tq == 0 and tq % V7X_LANES == 0 and s % build_rows == 0
    qspec = pl.BlockSpec((1, tq, HEAD_DIM), lambda bi, hi, qi: (bi, qi, hi))
    kspec = pl.BlockSpec((1, s, HEAD_DIM), lambda bi, hi, qi: (bi, 0, hi))
    return pl.pallas_call(
        functools.partial(_attn_prompt_kernel, tq=tq, build_rows=build_rows),
        grid=(b, nh, s // tq),
        in_specs=[qspec, kspec, kspec, qspec,
                  pl.BlockSpec((1, s, V7X_LANES), lambda bi, hi, qi: (bi, 0, 0))],
        out_specs=qspec,
        out_shape=jax.ShapeDtypeStruct((b, s, d), BF16),
        scratch_shapes=[
            pltpu.VMEM((tq, 2 * HEAD_DIM), BF16),
            pltpu.VMEM((s, 2 * HEAD_DIM), BF16),
            pltpu.VMEM((s, HEAD_DIM), BF16),
            pltpu.VMEM((tq, V7X_LANES), F32),
            pltpu.VMEM((tq, V7X_LANES), F32),
            pltpu.VMEM((tq, HEAD_DIM), F32),
        ],
        compiler_params=_params("arbitrary", "arbitrary", "arbitrary"),
        name="attn_prompt",
    )(q, k, v, g, c)


def _attn_sample_kernel(q_ref, kc_ref, vc_ref, kn_ref, vn_ref, g_ref, c_ref, o_ref):
    t = q_ref.shape[1]
    past = kc_ref.shape[1]
    nh = q_ref.shape[2] // HEAD_DIM
    dims = (((1,), (1,)), ((), ()))
    row = lax.broadcasted_iota(jnp.int32, (t, t), 0)
    col = lax.broadcasted_iota(jnp.int32, (t, t), 1)
    c_all = c_ref[0]
    c_past, c_new = c_all[0:past], c_all[past:past + t]
    for head in range(nh):
        cols = slice(head * HEAD_DIM, (head + 1) * HEAD_DIM)
        qa = jnp.concatenate([(q_ref[0, :, cols] * ATTN_SCALE).astype(BF16),
                              _aug_lanes(c_new, head, AUG_Q_LANES, AUG_K_LANES, 1.0).astype(BF16)], axis=1)
        kca = jnp.concatenate([kc_ref[0, :, cols].astype(BF16),
                               _aug_lanes(c_past, head, AUG_K_LANES, AUG_Q_LANES, -1.0).astype(BF16)], axis=1)
        kna = jnp.concatenate([kn_ref[0, :, cols].astype(BF16),
                               _aug_lanes(c_new, head, AUG_K_LANES, AUG_Q_LANES, -1.0).astype(BF16)], axis=1)
        s_past = lax.dot_general(qa, kca, dims, preferred_element_type=F32)
        s_new = jnp.where(col <= row, lax.dot_general(qa, kna, dims, preferred_element_type=F32), MASK_VALUE)
        m = jnp.maximum(jnp.max(s_past, axis=1, keepdims=True), jnp.max(s_new, axis=1, keepdims=True))
        p_past = jnp.exp(s_past - m)
        p_new = jnp.exp(s_new - m)
        l = jnp.sum(p_past, axis=1, keepdims=True) + jnp.sum(p_new, axis=1, keepdims=True)
        o = (jnp.dot(p_past.astype(BF16), vc_ref[0, :, cols].astype(BF16), preferred_element_type=F32)
             + jnp.dot(p_new.astype(BF16), vn_ref[0, :, cols].astype(BF16), preferred_element_type=F32)) / l
        o_ref[0, :, cols] = (o * jax.nn.sigmoid(g_ref[0, :, cols])).astype(BF16)


def _attn_sample(q, kc, vc, kn, vn, g, c):
    b, t, d = q.shape
    past = kc.shape[1]
    new = pl.BlockSpec((1, t, d), lambda bi: (bi, 0, 0))
    old = pl.BlockSpec((1, past, d), lambda bi: (bi, 0, 0))
    return pl.pallas_call(
        _attn_sample_kernel,
        grid=(b,),
        in_specs=[new, old, old, new, new, new,
                  pl.BlockSpec((1, past + t, V7X_LANES), lambda bi: (bi, 0, 0))],
        out_specs=new,
        out_shape=jax.ShapeDtypeStruct((b, t, d), BF16),
        compiler_params=_params("arbitrary"),
        name="attn_sample",
    )(q, kc, vc, kn, vn, g, c)


FFN_TM, FFN_TF = 512, 512
PROJ_TM, PROJ_TN_RNN, PROJ_TN_ATTN = 512, 640, 512
OUT_TM = 512
RGLRU_TT_PROMPT = 256
CUMSUM_ROWS_PROMPT, CUMSUM_ROWS_SAMPLE = 256, 64
ATTN_TQ, ATTN_BUILD_ROWS = 512, 512


def kernel(x_prompt, x_sample, state_conv, state_h, cache_k, cache_v, cache_logf, ffn_norm, ffn_w_in,
           ffn_w_out, a_norm, a_w_in, a_conv_w, a_conv_b, a_gate_w, a_gate_b, a_lambda, a_w_out, kv_norm,
           w_kv, w_f, b_f, b_norm, b_w_qg, b_w_o, final_norm):
    bp, sp, d = x_prompt.shape
    bs, ts, _ = x_sample.shape
    depth = ffn_w_in.shape[0]
    n_a = a_w_in.shape[0]
    n_heads = w_f.shape[1]
    past = cache_k.shape[1]

    ffn_w_in_b, ffn_w_out_b = ffn_w_in.astype(BF16), ffn_w_out.astype(BF16)
    a_w_in_b, a_gate_w_b, a_w_out_b = a_w_in.astype(BF16), a_gate_w.astype(BF16), a_w_out.astype(BF16)
    w_kv_b, b_w_qg_b, b_w_o_b = w_kv.astype(BF16), b_w_qg.astype(BF16), b_w_o.astype(BF16)
    wf_pad = jnp.pad(w_f, ((0, 0), (0, V7X_LANES - n_heads))).astype(BF16)
    bf_pad = jnp.pad(b_f, (0, V7X_LANES - n_heads)).reshape(1, V7X_LANES)

    streams = {"p": x_prompt.reshape(bp * sp, d), "s": x_sample.reshape(bs * ts, d)}
    batch = {"p": (bp, sp), "s": (bs, ts)}
    conv_out = {"p": [], "s": []}
    h_out = {"p": [], "s": []}
    kv = {}

    def ffn(x, l, half, apply_final=False):
        return _ffn(x, ffn_norm[l, half], ffn_w_in_b[l, half], ffn_w_out_b[l, half], final_norm,
                    apply_final=apply_final, tm=FFN_TM, tf=FFN_TF)

    for l in range(depth):
        for name in ("p", "s"):
            x = ffn(streams[name], l, 0)
            b, t = batch[name]
            if l < n_a:
                gate, u = _proj2(x, a_norm[l], a_w_in_b[l], tm=PROJ_TM, tn=PROJ_TN_RNN)
                c_rnn = u.shape[1]
                if name == "p":
                    conv0 = jnp.zeros((b, CONV_W - 1, c_rnn), F32)
                    h0 = jnp.zeros((b, c_rnn), F32)
                    tt = RGLRU_TT_PROMPT
                else:
                    conv0, h0, tt = state_conv[l], state_h[l], t
                hg, nconv, nh = _rglru(gate.reshape(b, t, c_rnn), u.reshape(b, t, c_rnn), conv0, h0,
                                       a_conv_w[l], a_conv_b[l], a_gate_w_b[l], a_gate_b[l], a_lambda[l], tt=tt)
                conv_out[name].append(nconv)
                h_out[name].append(nh.reshape(b, c_rnn))
                x = _out_proj(hg.reshape(b * t, c_rnn), a_w_out_b[l], x, tm=OUT_TM)
            else:
                jb = l - n_a
                q, g = _proj2(x, b_norm[jb], b_w_qg_b[jb], tm=PROJ_TM, tn=PROJ_TN_ATTN)
                k, v, f_pad = kv[name]
                if name == "p":
                    c = _cumsum(f_pad.reshape(b, t, V7X_LANES), rows=CUMSUM_ROWS_PROMPT)
                    og = _attn_prompt(q.reshape(b, t, d), k.reshape(b, t, d), v.reshape(b, t, d),
                                      g.reshape(b, t, d), c, tq=ATTN_TQ, build_rows=ATTN_BUILD_ROWS)
                else:
                    cache_f_pad = jnp.pad(cache_logf, ((0, 0), (0, 0), (0, V7X_LANES - n_heads)))
                    f_all = jnp.concatenate([cache_f_pad, f_pad.reshape(b, t, V7X_LANES)], axis=1)
                    c = _cumsum(f_all, rows=CUMSUM_ROWS_SAMPLE)
                    og = _attn_sample(q.reshape(b, t, d), cache_k.reshape(b, past, d),
                                      cache_v.reshape(b, past, d), k.reshape(b, t, d), v.reshape(b, t, d),
                                      g.reshape(b, t, d), c)
                x = _out_proj(og.reshape(b * t, d), b_w_o_b[jb], x, tm=OUT_TM)
            x = ffn(x, l, 1, apply_final=(l == depth - 1))
            if l == n_a - 1:
                kv[name] = _kv_proj(x, kv_norm, w_kv_b, wf_pad, bf_pad, tm=PROJ_TM, tn=PROJ_TN_ATTN)
            streams[name] = x

    def finish(name):
        b, t = batch[name]
        k, v, f_pad = kv[name]
        return (streams[name].reshape(b, t, d), jnp.stack(conv_out[name]), jnp.stack(h_out[name]),
                k.reshape(b, t, n_heads, HEAD_DIM), v.reshape(b, t, n_heads, HEAD_DIM),
                f_pad[:, :n_heads].reshape(b, t, n_heads))

    yp, convp, hp, kp, vp, fp = finish("p")
    ys, convs, hs, ks, vs, fs = finish("s")
    return (yp, ys, convp, hp, kp, vp, fp, convs, hs, ks, vs, fs)
```

```python
import functools

import jax
import jax.numpy as jnp
from jax import lax
from jax.experimental import pallas as pl
from jax.experimental.pallas import tpu as pltpu

F32 = jnp.float32
BF16 = jnp.bfloat16

RMS_EPS = 1e-6
LRU_C = 8.0
HEAD_DIM = 128
ATTN_SCALE = HEAD_DIM ** -0.5
MASK_VALUE = -1e30

V7X_LANES = 128
V7X_SUBLANES = 8
V7X_VMEM_BYTES = 64 * 1024 * 1024
VMEM_LIMIT_BYTES = V7X_VMEM_BYTES - 8 * 1024 * 1024

RNN_BLOCK = 256
CONV_W = 4
CONV_PAD_ROWS = V7X_SUBLANES

AUG_Q_LANES = (0, 1, 2)
AUG_K_LANES = (3, 4, 5)


def _params(*semantics):
    return pltpu.CompilerParams(dimension_semantics=semantics, vmem_limit_bytes=VMEM_LIMIT_BYTES)


def _rms(x, g):
    y = x * lax.rsqrt(jnp.mean(x * x, axis=-1, keepdims=True) + RMS_EPS)
    return y * g


def _ffn_kernel(x_ref, g_ref, wg_ref, wu_ref, wo_ref, fin_ref, o_ref, xn_ref, *, apply_final):
    j = pl.program_id(1)

    @pl.when(j == 0)
    def _():
        x = x_ref[...]
        xn_ref[...] = _rms(x, g_ref[...]).astype(BF16)
        o_ref[...] = x

    xn = xn_ref[...]
    gate = jnp.dot(xn, wg_ref[...], preferred_element_type=F32)
    up = jnp.dot(xn, wu_ref[...], preferred_element_type=F32)
    h = (0.5 * (gate * jax.nn.sigmoid(gate)) * up).astype(BF16)
    o_ref[...] += jnp.dot(h, wo_ref[...], preferred_element_type=F32)

    if apply_final:
        @pl.when(j == pl.num_programs(1) - 1)
        def _():
            o_ref[...] = _rms(o_ref[...], fin_ref[...])


def _ffn(x, g, w_in, w_out, fin, *, layer, half, apply_final, tm, tf):
    n, d = x.shape
    dff = w_out.shape[2]
    nj = dff // tf
    assert n % tm == 0 and dff % tf == 0 and w_in.shape[2:] == (d, 2 * dff)
    return pl.pallas_call(
        functools.partial(_ffn_kernel, apply_final=apply_final),
        grid=(n // tm, nj),
        in_specs=[
            pl.BlockSpec((tm, d), lambda i, j: (i, 0)),
            pl.BlockSpec((1, d), lambda i, j: (0, 0)),
            pl.BlockSpec((None, None, d, tf), lambda i, j: (layer, half, 0, j)),
            pl.BlockSpec((None, None, d, tf), lambda i, j: (layer, half, 0, j + nj)),
            pl.BlockSpec((None, None, tf, d), lambda i, j: (layer, half, j, 0)),
            pl.BlockSpec((1, d), lambda i, j: (0, 0)),
        ],
        out_specs=pl.BlockSpec((tm, d), lambda i, j: (i, 0)),
        out_shape=jax.ShapeDtypeStruct((n, d), F32),
        scratch_shapes=[pltpu.VMEM((tm, d), BF16)],
        compiler_params=_params("arbitrary", "arbitrary"),
        name="ffn",
    )(x, g.reshape(1, d), w_in, w_in, w_out, fin.reshape(1, d))


def _proj2_kernel(x_ref, g_ref, wa_ref, wb_ref, oa_ref, ob_ref, xn_ref):
    @pl.when(pl.program_id(1) == 0)
    def _():
        xn_ref[...] = _rms(x_ref[...], g_ref[...]).astype(BF16)

    xn = xn_ref[...]
    oa_ref[...] = jnp.dot(xn, wa_ref[...], preferred_element_type=F32)
    ob_ref[...] = jnp.dot(xn, wb_ref[...], preferred_element_type=F32)


def _proj2(x, g, w, *, tm, tn):
    n, d = x.shape
    half = w.shape[1] // 2
    nj = half // tn
    assert n % tm == 0 and half % tn == 0
    out = jax.ShapeDtypeStruct((n, half), F32)
    return pl.pallas_call(
        _proj2_kernel,
        grid=(n // tm, nj),
        in_specs=[
            pl.BlockSpec((tm, d), lambda i, j: (i, 0)),
            pl.BlockSpec((1, d), lambda i, j: (0, 0)),
            pl.BlockSpec((d, tn), lambda i, j: (0, j)),
            pl.BlockSpec((d, tn), lambda i, j: (0, j + nj)),
        ],
        out_specs=[pl.BlockSpec((tm, tn), lambda i, j: (i, j))] * 2,
        out_shape=[out, out],
        scratch_shapes=[pltpu.VMEM((tm, d), BF16)],
        compiler_params=_params("arbitrary", "arbitrary"),
        name="proj2",
    )(x, g.reshape(1, d), w, w)


def _log_sigmoid(z):
    return jnp.minimum(z, 0.0) - jnp.log1p(jnp.exp(-jnp.abs(z)))


def _kv_kernel(x_ref, g_ref, wk_ref, wv_ref, wf_ref, bf_ref, k_ref, v_ref, f_ref, xn_ref):
    @pl.when(pl.program_id(1) == 0)
    def _():
        xn = _rms(x_ref[...], g_ref[...]).astype(BF16)
        xn_ref[...] = xn
        z = jnp.dot(xn, wf_ref[...], preferred_element_type=F32) + bf_ref[...]
        f_ref[...] = _log_sigmoid(z)

    xn = xn_ref[...]
    k_ref[...] = jnp.dot(xn, wk_ref[...], preferred_element_type=F32)
    v_ref[...] = jnp.dot(xn, wv_ref[...], preferred_element_type=F32)


def _kv_proj(x, g, w_kv, wf_pad, bf_pad, *, tm, tn):
    n, d = x.shape
    half = w_kv.shape[1] // 2
    nj = half // tn
    assert n % tm == 0 and half % tn == 0
    out = jax.ShapeDtypeStruct((n, half), F32)
    return pl.pallas_call(
        _kv_kernel,
        grid=(n // tm, nj),
        in_specs=[
            pl.BlockSpec((tm, d), lambda i, j: (i, 0)),
            pl.BlockSpec((1, d), lambda i, j: (0, 0)),
            pl.BlockSpec((d, tn), lambda i, j: (0, j)),
            pl.BlockSpec((d, tn), lambda i, j: (0, j + nj)),
            pl.BlockSpec((d, V7X_LANES), lambda i, j: (0, 0)),
            pl.BlockSpec((1, V7X_LANES), lambda i, j: (0, 0)),
        ],
        out_specs=[
            pl.BlockSpec((tm, tn), lambda i, j: (i, j)),
            pl.BlockSpec((tm, tn), lambda i, j: (i, j)),
            pl.BlockSpec((tm, V7X_LANES), lambda i, j: (i, 0)),
        ],
        out_shape=[out, out, jax.ShapeDtypeStruct((n, V7X_LANES), F32)],
        scratch_shapes=[pltpu.VMEM((tm, d), BF16)],
        compiler_params=_params("arbitrary", "arbitrary"),
        name="kv_proj",
    )(x, g.reshape(1, d), w_kv, w_kv, wf_pad, bf_pad)


def _out_proj_kernel(a_ref, w_ref, x_ref, o_ref):
    o_ref[...] = x_ref[...] + jnp.dot(a_ref[...], w_ref[...], preferred_element_type=F32)


def _out_proj(a, w, x, *, tm):
    n, k = a.shape
    d = w.shape[1]
    assert n % tm == 0
    return pl.pallas_call(
        _out_proj_kernel,
        grid=(n // tm,),
        in_specs=[
            pl.BlockSpec((tm, k), lambda i: (i, 0)),
            pl.BlockSpec((k, d), lambda i: (0, 0)),
            pl.BlockSpec((tm, d), lambda i: (i, 0)),
        ],
        out_specs=pl.BlockSpec((tm, d), lambda i: (i, 0)),
        out_shape=jax.ShapeDtypeStruct((n, d), F32),
        compiler_params=_params("arbitrary"),
        name="out_proj",
    )(a, w, x)


def _gelu_tanh(x):
    return 0.5 * x * (1.0 + jnp.tanh(0.7978845608028654 * (x + 0.044715 * (x * x * x))))


def _softplus(y):
    return jnp.maximum(y, 0.0) + jnp.log1p(jnp.exp(-jnp.abs(y)))


def _scan_rows(a_ref, b_ref, cols, h_in, tt):
    width = cols.stop - cols.start
    row = lax.broadcasted_iota(jnp.int32, (V7X_SUBLANES, width), 0)

    def group(gi, hc):
        r0 = pl.multiple_of(gi * V7X_SUBLANES, V7X_SUBLANES)
        a = a_ref[pl.ds(r0, V7X_SUBLANES), cols]
        b = b_ref[pl.ds(r0, V7X_SUBLANES), cols]
        for s in (1, 2, 4):
            keep = row >= s
            a_sh = pltpu.roll(a, s, 0)
            b_sh = pltpu.roll(b, s, 0)
            b = jnp.where(keep, a * b_sh + b, b)
            a = jnp.where(keep, a * a_sh, a)
        h = a * hc + b
        b_ref[pl.ds(r0, V7X_SUBLANES), cols] = h
        return h[V7X_SUBLANES - 1:V7X_SUBLANES, :]

    return lax.fori_loop(0, tt // V7X_SUBLANES, group, h_in, unroll=2)


def _rglru_kernel(gate_ref, u_ref, conv0_ref, h0_ref, cw_ref, cb_ref, gw_ref, gb_ref, lam_ref,
                  hg_ref, nconv_ref, nh_ref, ubuf, a_s, b_s, hcar, *, tt):
    c = u_ref.shape[-1]
    pad = CONV_PAD_ROWS

    @pl.when(pl.program_id(1) == 0)
    def _():
        ubuf[0:pad, :] = conv0_ref[0]
        hcar[...] = h0_ref[0]

    ubuf[pad:pad + tt, :] = u_ref[0]

    for blk in range(c // RNN_BLOCK):
        cols = slice(blk * RNN_BLOCK, (blk + 1) * RNN_BLOCK)
        uc = cb_ref[:, cols]
        for k in range(CONV_W):
            uc = uc + ubuf[pad - (CONV_W - 1 - k):pad - (CONV_W - 1 - k) + tt, cols] * cw_ref[k:k + 1, cols]
        ub = uc.astype(BF16)
        r = jax.nn.sigmoid(jnp.dot(ub, gw_ref[0, blk], preferred_element_type=F32) + gb_ref[0:1, cols])
        i = jax.nn.sigmoid(jnp.dot(ub, gw_ref[1, blk], preferred_element_type=F32) + gb_ref[1:2, cols])
        log_a = (-LRU_C * r) * _softplus(-lam_ref[:, cols])
        a = jnp.exp(log_a)
        a_s[:, cols] = a
        b_s[:, cols] = jnp.sqrt(1.0 - a * a) * (i * uc)
        h_last = _scan_rows(a_s, b_s, cols, hcar[:, cols], tt)
        hcar[:, cols] = h_last
        hg_ref[0, :, cols] = (b_s[:, cols] * _gelu_tanh(gate_ref[0, :, cols])).astype(BF16)

    tail = ubuf[pad + tt - (CONV_W - 1):pad + tt, :]
    nconv_ref[0] = tail
    ubuf[pad - (CONV_W - 1):pad, :] = tail
    nh_ref[0] = hcar[...]


def _rglru(gate, u, conv0, h0, cw, cb, gw, gb, lam, *, tt):
    b, t, c = u.shape
    assert t % tt == 0 and tt % V7X_SUBLANES == 0 and tt >= CONV_W - 1
    nblk = c // RNN_BLOCK
    conv0_pad = jnp.pad(conv0, ((0, 0), (CONV_PAD_ROWS - (CONV_W - 1), 0), (0, 0)))
    return pl.pallas_call(
        functools.partial(_rglru_kernel, tt=tt),
        grid=(b, t // tt),
        in_specs=[
            pl.BlockSpec((1, tt, c), lambda bi, ti: (bi, ti, 0)),
            pl.BlockSpec((1, tt, c), lambda bi, ti: (bi, ti, 0)),
            pl.BlockSpec((1, CONV_PAD_ROWS, c), lambda bi, ti: (bi, 0, 0)),
            pl.BlockSpec((1, 1, c), lambda bi, ti: (bi, 0, 0)),
            pl.BlockSpec((CONV_W, c), lambda bi, ti: (0, 0)),
            pl.BlockSpec((1, c), lambda bi, ti: (0, 0)),
            pl.BlockSpec((2, nblk, RNN_BLOCK, RNN_BLOCK), lambda bi, ti: (0, 0, 0, 0)),
            pl.BlockSpec((2, c), lambda bi, ti: (0, 0)),
            pl.BlockSpec((1, c), lambda bi, ti: (0, 0)),
        ],
        out_specs=[
            pl.BlockSpec((1, tt, c), lambda bi, ti: (bi, ti, 0)),
            pl.BlockSpec((1, CONV_W - 1, c), lambda bi, ti: (bi, 0, 0)),
            pl.BlockSpec((1, 1, c), lambda bi, ti: (bi, 0, 0)),
        ],
        out_shape=[
            jax.ShapeDtypeStruct((b, t, c), BF16),
            jax.ShapeDtypeStruct((b, CONV_W - 1, c), F32),
            jax.ShapeDtypeStruct((b, 1, c), F32),
        ],
        scratch_shapes=[
            pltpu.VMEM((CONV_PAD_ROWS + tt, c), F32),
            pltpu.VMEM((tt, c), F32),
            pltpu.VMEM((tt, c), F32),
            pltpu.VMEM((1, c), F32),
        ],
        compiler_params=_params("arbitrary", "arbitrary"),
        name="rglru",
    )(gate, u, conv0_pad, h0.reshape(b, 1, c), cw, cb.reshape(1, c), gw, gb, lam.reshape(1, c))


def _cumsum_kernel(x_ref, o_ref, *, rows):
    s = x_ref.shape[1]
    r = lax.broadcasted_iota(jnp.int32, (rows, rows), 0)
    q = lax.broadcasted_iota(jnp.int32, (rows, rows), 1)
    tril = (q <= r).astype(F32)

    def block(bi, carry):
        r0 = pl.multiple_of(bi * rows, rows)
        cs = jnp.dot(tril, x_ref[0, pl.ds(r0, rows), :], preferred_element_type=F32,
                     precision=lax.Precision.HIGHEST) + carry
        o_ref[0, pl.ds(r0, rows), :] = cs
        return cs[rows - 1:rows, :]

    lax.fori_loop(0, s // rows, block, jnp.zeros((1, x_ref.shape[2]), F32))


def _cumsum(x, *, rows):
    b, s, w = x.shape
    assert s % rows == 0
    return pl.pallas_call(
        functools.partial(_cumsum_kernel, rows=rows),
        grid=(b,),
        in_specs=[pl.BlockSpec((1, s, w), lambda bi: (bi, 0, 0))],
        out_specs=pl.BlockSpec((1, s, w), lambda bi: (bi, 0, 0)),
        out_shape=jax.ShapeDtypeStruct((b, s, w), F32),
        compiler_params=_params("arbitrary"),
        name="cumsum",
    )(x)


def _split3(c):
    hi = c.astype(BF16)
    r1 = c - hi.astype(F32)
    mid = r1.astype(BF16)
    lo = (r1 - mid.astype(F32)).astype(BF16)
    return hi, mid, lo


def _aug_lanes(c, head, piece_lanes, one_lanes, sign):
    w = c.shape[1]
    src = lax.broadcasted_iota(jnp.int32, (w, V7X_LANES), 0)
    dst = lax.broadcasted_iota(jnp.int32, (w, V7X_LANES), 1)
    out = None
    for piece, lane in zip(_split3(c), piece_lanes):
        route = jnp.where((src == head) & (dst == lane), sign, 0.0).astype(BF16)
        term = jnp.dot(piece, route, preferred_element_type=F32)
        out = term if out is None else out + term
    lane_id = lax.broadcasted_iota(jnp.int32, out.shape, 1)
    ones = (lane_id >= one_lanes[0]) & (lane_id <= one_lanes[-1])
    return jnp.where(ones, 1.0, out)


def _attn_prompt_kernel(q_ref, k_ref, v_ref, g_ref, c_ref, o_ref,
                        qa_ref, ka_ref, vt_ref, m_ref, l_ref, acc_ref, *, tile):
    head = pl.program_id(1)
    qi = pl.program_id(2)
    n_kt = vt_ref.shape[0]
    dims = (((1,), (1,)), ((), ()))

    @pl.when(qi == 0)
    def _():
        def build(bi, carry):
            rows = pl.ds(pl.multiple_of(bi * tile, tile), tile)
            ka_ref[rows, 0:HEAD_DIM] = k_ref[0, rows, :].astype(BF16)
            ka_ref[rows, HEAD_DIM:] = _aug_lanes(c_ref[0, rows, :], head, AUG_K_LANES, AUG_Q_LANES,
                                                 -1.0).astype(BF16)
            vt_ref[bi] = v_ref[0, rows, :].T.astype(BF16)
            return carry
        lax.fori_loop(0, n_kt, build, 0)

    q0 = pl.multiple_of(qi * (2 * tile), 2 * tile)
    qa_ref[:, 0:HEAD_DIM] = (q_ref[0] * ATTN_SCALE).astype(BF16)
    qa_ref[:, HEAD_DIM:] = _aug_lanes(c_ref[0, pl.ds(q0, 2 * tile), :], head, AUG_Q_LANES, AUG_K_LANES,
                                      1.0).astype(BF16)
    m_ref[...] = jnp.full(m_ref.shape, MASK_VALUE, F32)
    l_ref[...] = jnp.zeros(l_ref.shape, F32)
    acc_ref[...] = jnp.zeros(acc_ref.shape, F32)

    def scores(half, kj, masked):
        k0 = pl.multiple_of(kj * tile, tile)
        st = lax.dot_general(ka_ref[pl.ds(k0, tile), :], qa_ref[half * tile:(half + 1) * tile, :], dims,
                             preferred_element_type=F32)
        if masked:
            key = lax.broadcasted_iota(jnp.int32, (tile, tile), 0)
            qry = lax.broadcasted_iota(jnp.int32, (tile, tile), 1)
            st = jnp.where(key <= qry, st, MASK_VALUE)
        return st

    def update(half, kj, st):
        m_prev = m_ref[half]
        m_new = jnp.maximum(m_prev, jnp.max(st, axis=0, keepdims=True))
        alpha = jnp.exp(m_prev - m_new)
        p = jnp.exp(st - m_new)
        l_ref[half] = alpha * l_ref[half] + jnp.sum(p, axis=0, keepdims=True)
        acc_ref[half] = alpha * acc_ref[half] + jnp.dot(vt_ref[kj], p.astype(BF16),
                                                        preferred_element_type=F32)
        m_ref[half] = m_new

    def both(kj, carry):
        st0 = scores(0, kj, False)
        st1 = scores(1, kj, False)
        update(0, kj, st0)
        update(1, kj, st1)
        return carry

    lax.fori_loop(0, 2 * qi, both, 0)
    st0 = scores(0, 2 * qi, True)
    st1 = scores(1, 2 * qi, False)
    update(0, 2 * qi, st0)
    update(1, 2 * qi, st1)
    update(1, 2 * qi + 1, scores(1, 2 * qi + 1, True))
    for half in range(2):
        rows = slice(half * tile, (half + 1) * tile)
        o = (acc_ref[half] / l_ref[half]).T
        o_ref[0, rows, :] = (o * jax.nn.sigmoid(g_ref[0, rows, :])).astype(BF16)


def _attn_prompt(q, k, v, g, c, *, tile):
    b, s, d = q.shape
    nh = d // HEAD_DIM
    assert s % (2 * tile) == 0 and tile % V7X_LANES == 0
    qspec = pl.BlockSpec((1, 2 * tile, HEAD_DIM), lambda bi, hi, qi: (bi, qi, hi))
    kspec = pl.BlockSpec((1, s, HEAD_DIM), lambda bi, hi, qi: (bi, 0, hi))
    return pl.pallas_call(
        functools.partial(_attn_prompt_kernel, tile=tile),
        grid=(b, nh, s // (2 * tile)),
        in_specs=[qspec, kspec, kspec, qspec,
                  pl.BlockSpec((1, s, V7X_LANES), lambda bi, hi, qi: (bi, 0, 0))],
        out_specs=qspec,
        out_shape=jax.ShapeDtypeStruct((b, s, d), BF16),
        scratch_shapes=[
            pltpu.VMEM((2 * tile, 2 * HEAD_DIM), BF16),
            pltpu.VMEM((s, 2 * HEAD_DIM), BF16),
            pltpu.VMEM((s // tile, HEAD_DIM, tile), BF16),
            pltpu.VMEM((2, 1, tile), F32),
            pltpu.VMEM((2, 1, tile), F32),
            pltpu.VMEM((2, HEAD_DIM, tile), F32),
        ],
        compiler_params=_params("arbitrary", "arbitrary", "arbitrary"),
        name="attn_prompt",
    )(q, k, v, g, c)


def _attn_sample_kernel(q_ref, kc_ref, vc_ref, kn_ref, vn_ref, g_ref, c_ref, o_ref):
    t = q_ref.shape[1]
    past = kc_ref.shape[1]
    nh = q_ref.shape[2] // HEAD_DIM
    dims = (((1,), (1,)), ((), ()))
    row = lax.broadcasted_iota(jnp.int32, (t, t), 0)
    col = lax.broadcasted_iota(jnp.int32, (t, t), 1)
    c_all = c_ref[0]
    c_past, c_new = c_all[0:past], c_all[past:past + t]
    for head in range(nh):
        cols = slice(head * HEAD_DIM, (head + 1) * HEAD_DIM)
        qa = jnp.concatenate([(q_ref[0, :, cols] * ATTN_SCALE).astype(BF16),
                              _aug_lanes(c_new, head, AUG_Q_LANES, AUG_K_LANES, 1.0).astype(BF16)], axis=1)
        kca = jnp.concatenate([kc_ref[0, :, head, :].astype(BF16),
                               _aug_lanes(c_past, head, AUG_K_LANES, AUG_Q_LANES, -1.0).astype(BF16)], axis=1)
        kna = jnp.concatenate([kn_ref[0, :, cols].astype(BF16),
                               _aug_lanes(c_new, head, AUG_K_LANES, AUG_Q_LANES, -1.0).astype(BF16)], axis=1)
        s_past = lax.dot_general(qa, kca, dims, preferred_element_type=F32)
        s_new = jnp.where(col <= row, lax.dot_general(qa, kna, dims, preferred_element_type=F32), MASK_VALUE)
        m = jnp.maximum(jnp.max(s_past, axis=1, keepdims=True), jnp.max(s_new, axis=1, keepdims=True))
        p_past = jnp.exp(s_past - m)
        p_new = jnp.exp(s_new - m)
        l = jnp.sum(p_past, axis=1, keepdims=True) + jnp.sum(p_new, axis=1, keepdims=True)
        o = (jnp.dot(p_past.astype(BF16), vc_ref[0, :, head, :].astype(BF16), preferred_element_type=F32)
             + jnp.dot(p_new.astype(BF16), vn_ref[0, :, cols].astype(BF16), preferred_element_type=F32)) / l
        o_ref[0, :, cols] = (o * jax.nn.sigmoid(g_ref[0, :, cols])).astype(BF16)


def _attn_sample(q, kc, vc, kn, vn, g, c):
    b, t, d = q.shape
    _, past, nh, hd = kc.shape
    new = pl.BlockSpec((1, t, d), lambda bi: (bi, 0, 0))
    old = pl.BlockSpec((1, past, nh, hd), lambda bi: (bi, 0, 0, 0))
    return pl.pallas_call(
        _attn_sample_kernel,
        grid=(b,),
        in_specs=[new, old, old, new, new, new,
                  pl.BlockSpec((1, past + t, V7X_LANES), lambda bi: (bi, 0, 0))],
        out_specs=new,
        out_shape=jax.ShapeDtypeStruct((b, t, d), BF16),
        compiler_params=_params("arbitrary"),
        name="attn_sample",
    )(q, kc, vc, kn, vn, g, c)


FFN_TM, FFN_TF = 512, 512
PROJ_TM, PROJ_TN_RNN, PROJ_TN_ATTN = 512, 640, 512
OUT_TM = 512
RGLRU_TT_PROMPT = 256
CUMSUM_ROWS_PROMPT, CUMSUM_ROWS_SAMPLE = 256, 64
ATTN_TILE = 512


def kernel(x_prompt, x_sample, state_conv, state_h, cache_k, cache_v, cache_logf, ffn_norm, ffn_w_in,
           ffn_w_out, a_norm, a_w_in, a_conv_w, a_conv_b, a_gate_w, a_gate_b, a_lambda, a_w_out, kv_norm,
           w_kv, w_f, b_f, b_norm, b_w_qg, b_w_o, final_norm):
    bp, sp, d = x_prompt.shape
    bs, ts, _ = x_sample.shape
    depth = ffn_w_in.shape[0]
    n_a = a_w_in.shape[0]
    n_heads = w_f.shape[1]
    past = cache_k.shape[1]

    ffn_w_in_b, ffn_w_out_b = ffn_w_in.astype(BF16), ffn_w_out.astype(BF16)
    a_w_in_b, a_gate_w_b, a_w_out_b = a_w_in.astype(BF16), a_gate_w.astype(BF16), a_w_out.astype(BF16)
    w_kv_b, b_w_qg_b, b_w_o_b = w_kv.astype(BF16), b_w_qg.astype(BF16), b_w_o.astype(BF16)
    wf_pad = jnp.pad(w_f, ((0, 0), (0, V7X_LANES - n_heads))).astype(BF16)
    bf_pad = jnp.pad(b_f, (0, V7X_LANES - n_heads)).reshape(1, V7X_LANES)

    streams = {"p": x_prompt.reshape(bp * sp, d), "s": x_sample.reshape(bs * ts, d)}
    batch = {"p": (bp, sp), "s": (bs, ts)}
    conv_out = {"p": [], "s": []}
    h_out = {"p": [], "s": []}
    kv = {}

    def ffn(x, l, half, apply_final=False):
        return _ffn(x, ffn_norm[l, half], ffn_w_in_b, ffn_w_out_b, final_norm, layer=l, half=half,
                    apply_final=apply_final, tm=FFN_TM, tf=FFN_TF)

    for l in range(depth):
        for name in ("p", "s"):
            x = ffn(streams[name], l, 0)
            b, t = batch[name]
            if l < n_a:
                gate, u = _proj2(x, a_norm[l], a_w_in_b[l], tm=PROJ_TM, tn=PROJ_TN_RNN)
                c_rnn = u.shape[1]
                if name == "p":
                    conv0 = jnp.zeros((b, CONV_W - 1, c_rnn), F32)
                    h0 = jnp.zeros((b, c_rnn), F32)
                    tt = RGLRU_TT_PROMPT
                else:
                    conv0, h0, tt = state_conv[l], state_h[l], t
                hg, nconv, nh = _rglru(gate.reshape(b, t, c_rnn), u.reshape(b, t, c_rnn), conv0, h0,
                                       a_conv_w[l], a_conv_b[l], a_gate_w_b[l], a_gate_b[l], a_lambda[l], tt=tt)
                conv_out[name].append(nconv)
                h_out[name].append(nh.reshape(b, c_rnn))
                x = _out_proj(hg.reshape(b * t, c_rnn), a_w_out_b[l], x, tm=OUT_TM)
            else:
                jb = l - n_a
                q, g = _proj2(x, b_norm[jb], b_w_qg_b[jb], tm=PROJ_TM, tn=PROJ_TN_ATTN)
                k, v, f_pad = kv[name]
                if name == "p":
                    c = _cumsum(f_pad.reshape(b, t, V7X_LANES), rows=CUMSUM_ROWS_PROMPT)
                    og = _attn_prompt(q.reshape(b, t, d), k.reshape(b, t, d), v.reshape(b, t, d),
                                      g.reshape(b, t, d), c, tile=ATTN_TILE)
                else:
                    cache_f_pad = jnp.pad(cache_logf, ((0, 0), (0, 0), (0, V7X_LANES - n_heads)))
                    f_all = jnp.concatenate([cache_f_pad, f_pad.reshape(b, t, V7X_LANES)], axis=1)
                    c = _cumsum(f_all, rows=CUMSUM_ROWS_SAMPLE)
                    og = _attn_sample(q.reshape(b, t, d), cache_k, cache_v, k.reshape(b, t, d),
                                      v.reshape(b, t, d), g.reshape(b, t, d), c)
                x = _out_proj(og.reshape(b * t, d), b_w_o_b[jb], x, tm=OUT_TM)
            x = ffn(x, l, 1, apply_final=(l == depth - 1))
            if l == n_a - 1:
                kv[name] = _kv_proj(x, kv_norm, w_kv_b, wf_pad, bf_pad, tm=PROJ_TM, tn=PROJ_TN_ATTN)
            streams[name] = x

    def finish(name):
        b, t = batch[name]
        k, v, f_pad = kv[name]
        return (streams[name].reshape(b, t, d), jnp.stack(conv_out[name]), jnp.stack(h_out[name]),
                k.reshape(b, t, n_heads, HEAD_DIM), v.reshape(b, t, n_heads, HEAD_DIM),
                f_pad[:, :n_heads].reshape(b, t, n_heads))

    yp, convp, hp, kp, vp, fp = finish("p")
    ys, convs, hs, ks, vs, fs = finish("s")
    return (yp, ys, convp, hp, kp, vp, fp, convs, hs, ks, vs, fs)
```

```python
import functools

import jax
import jax.numpy as jnp
from jax import lax
from jax.experimental import pallas as pl
from jax.experimental.pallas import tpu as pltpu

F32 = jnp.float32
BF16 = jnp.bfloat16

RMS_EPS = 1e-6
LRU_C = 8.0
HEAD_DIM = 128
ATTN_SCALE = HEAD_DIM ** -0.5
MASK_VALUE = -1e30

V7X_LANES = 128
V7X_SUBLANES = 8
V7X_VMEM_BYTES = 64 * 1024 * 1024
VMEM_LIMIT_BYTES = V7X_VMEM_BYTES - 8 * 1024 * 1024

RNN_BLOCK = 256
SCAN_GROUP = 8
CONV_W = 4
CONV_PAD_ROWS = V7X_SUBLANES

AUG_Q_LANES = (0, 1, 2)
AUG_K_LANES = (3, 4, 5)


def _params(*semantics):
    return pltpu.CompilerParams(dimension_semantics=semantics, vmem_limit_bytes=VMEM_LIMIT_BYTES)


def _rms(x, g):
    y = x * lax.rsqrt(jnp.mean(x * x, axis=-1, keepdims=True) + RMS_EPS)
    return y * g


def _ffn_kernel(x_ref, g_ref, wg_ref, wu_ref, wo_ref, fin_ref, o_ref, xn_ref, *, apply_final, row_chunk,
                col_chunk):
    j = pl.program_id(1)
    tm, d = x_ref.shape

    def by_rows(fn):
        def body(ri, carry):
            fn(pl.ds(pl.multiple_of(ri * row_chunk, row_chunk), row_chunk))
            return carry
        lax.fori_loop(0, tm // row_chunk, body, 0)

    @pl.when(j == 0)
    def _():
        def prep(rows):
            x = x_ref[rows, :]
            xn_ref[rows, :] = _rms(x, g_ref[...]).astype(BF16)
            o_ref[rows, :] = x
        by_rows(prep)

    xn = xn_ref[...]
    gate = jnp.dot(xn, wg_ref[...], preferred_element_type=F32)
    up = jnp.dot(xn, wu_ref[...], preferred_element_type=F32)
    h = (0.5 * (gate * jax.nn.sigmoid(gate)) * up).astype(BF16)
    for c in range(d // col_chunk):
        cols = slice(c * col_chunk, (c + 1) * col_chunk)
        o_ref[:, cols] += jnp.dot(h, wo_ref[:, cols], preferred_element_type=F32)

    if apply_final:
        @pl.when(j == pl.num_programs(1) - 1)
        def _():
            def norm(rows):
                o_ref[rows, :] = _rms(o_ref[rows, :], fin_ref[...])
            by_rows(norm)


def _ffn(x, g, w_in, w_out, fin, *, layer, half, apply_final, tm, tf):
    n, d = x.shape
    dff = w_out.shape[2]
    nj = dff // tf
    assert n % tm == 0 and dff % tf == 0 and w_in.shape[2:] == (d, 2 * dff)
    row_chunk, col_chunk = min(NORM_ROW_CHUNK, tm), min(FFN_COL_CHUNK, d)
    assert tm % row_chunk == 0 and d % col_chunk == 0
    return pl.pallas_call(
        functools.partial(_ffn_kernel, apply_final=apply_final, row_chunk=row_chunk, col_chunk=col_chunk),
        grid=(n // tm, nj),
        in_specs=[
            pl.BlockSpec((tm, d), lambda i, j: (i, 0), pipeline_mode=pl.Buffered(1)),
            pl.BlockSpec((1, d), lambda i, j: (0, 0)),
            pl.BlockSpec((None, None, d, tf), lambda i, j: (layer, half, 0, j)),
            pl.BlockSpec((None, None, d, tf), lambda i, j: (layer, half, 0, j + nj)),
            pl.BlockSpec((None, None, tf, d), lambda i, j: (layer, half, j, 0)),
            pl.BlockSpec((1, d), lambda i, j: (0, 0)),
        ],
        out_specs=pl.BlockSpec((tm, d), lambda i, j: (i, 0)),
        out_shape=jax.ShapeDtypeStruct((n, d), F32),
        scratch_shapes=[pltpu.VMEM((tm, d), BF16)],
        compiler_params=_params("arbitrary", "arbitrary"),
        name="ffn",
    )(x, g.reshape(1, d), w_in, w_in, w_out, fin.reshape(1, d))


def _norm_rows(x_ref, g_ref, xn_ref, row_chunk):
    def body(ri, carry):
        rows = pl.ds(pl.multiple_of(ri * row_chunk, row_chunk), row_chunk)
        xn_ref[rows, :] = _rms(x_ref[rows, :], g_ref[...]).astype(BF16)
        return carry
    lax.fori_loop(0, x_ref.shape[0] // row_chunk, body, 0)


def _proj2_kernel(x_ref, g_ref, wa_ref, wb_ref, oa_ref, ob_ref, xn_ref, *, a_scale, row_chunk):
    @pl.when(pl.program_id(1) == 0)
    def _():
        _norm_rows(x_ref, g_ref, xn_ref, row_chunk)

    xn = xn_ref[...]
    a = jnp.dot(xn, wa_ref[...], preferred_element_type=F32)
    if a_scale is not None:
        a = a * a_scale
    oa_ref[...] = a.astype(oa_ref.dtype)
    ob_ref[...] = jnp.dot(xn, wb_ref[...], preferred_element_type=F32)


def _proj2(x, g, w, *, tm, tn, a_scale=None, a_dtype=F32):
    n, d = x.shape
    half = w.shape[1] // 2
    nj = half // tn
    row_chunk = min(NORM_ROW_CHUNK, tm)
    assert n % tm == 0 and half % tn == 0 and tm % row_chunk == 0
    return pl.pallas_call(
        functools.partial(_proj2_kernel, a_scale=a_scale, row_chunk=row_chunk),
        grid=(n // tm, nj),
        in_specs=[
            pl.BlockSpec((tm, d), lambda i, j: (i, 0), pipeline_mode=pl.Buffered(1)),
            pl.BlockSpec((1, d), lambda i, j: (0, 0)),
            pl.BlockSpec((d, tn), lambda i, j: (0, j)),
            pl.BlockSpec((d, tn), lambda i, j: (0, j + nj)),
        ],
        out_specs=[pl.BlockSpec((tm, tn), lambda i, j: (i, j))] * 2,
        out_shape=[jax.ShapeDtypeStruct((n, half), a_dtype), jax.ShapeDtypeStruct((n, half), F32)],
        scratch_shapes=[pltpu.VMEM((tm, d), BF16)],
        compiler_params=_params("arbitrary", "arbitrary"),
        name="proj2",
    )(x, g.reshape(1, d), w, w)


def _log_sigmoid(z):
    return jnp.minimum(z, 0.0) - jnp.log1p(jnp.exp(-jnp.abs(z)))


def _kv_kernel(x_ref, g_ref, wk_ref, wv_ref, wf_ref, bf_ref, k_ref, v_ref, f_ref, xn_ref, *, row_chunk):
    @pl.when(pl.program_id(1) == 0)
    def _():
        _norm_rows(x_ref, g_ref, xn_ref, row_chunk)
        z = jnp.dot(xn_ref[...], wf_ref[...], preferred_element_type=F32) + bf_ref[...]
        f_ref[...] = _log_sigmoid(z)

    xn = xn_ref[...]
    k_ref[...] = jnp.dot(xn, wk_ref[...], preferred_element_type=F32)
    v_ref[...] = jnp.dot(xn, wv_ref[...], preferred_element_type=F32)


def _kv_proj(x, g, w_kv, wf_pad, bf_pad, *, tm, tn):
    n, d = x.shape
    half = w_kv.shape[1] // 2
    nj = half // tn
    row_chunk = min(NORM_ROW_CHUNK, tm)
    assert n % tm == 0 and half % tn == 0 and tm % row_chunk == 0
    out = jax.ShapeDtypeStruct((n, half), F32)
    return pl.pallas_call(
        functools.partial(_kv_kernel, row_chunk=row_chunk),
        grid=(n // tm, nj),
        in_specs=[
            pl.BlockSpec((tm, d), lambda i, j: (i, 0), pipeline_mode=pl.Buffered(1)),
            pl.BlockSpec((1, d), lambda i, j: (0, 0)),
            pl.BlockSpec((d, tn), lambda i, j: (0, j)),
            pl.BlockSpec((d, tn), lambda i, j: (0, j + nj)),
            pl.BlockSpec((d, V7X_LANES), lambda i, j: (0, 0)),
            pl.BlockSpec((1, V7X_LANES), lambda i, j: (0, 0)),
        ],
        out_specs=[
            pl.BlockSpec((tm, tn), lambda i, j: (i, j)),
            pl.BlockSpec((tm, tn), lambda i, j: (i, j)),
            pl.BlockSpec((tm, V7X_LANES), lambda i, j: (i, 0)),
        ],
        out_shape=[out, out, jax.ShapeDtypeStruct((n, V7X_LANES), F32)],
        scratch_shapes=[pltpu.VMEM((tm, d), BF16)],
        compiler_params=_params("arbitrary", "arbitrary"),
        name="kv_proj",
    )(x, g.reshape(1, d), w_kv, w_kv, wf_pad, bf_pad)


def _out_proj_kernel(a_ref, w_ref, x_ref, o_ref):
    o_ref[...] = x_ref[...] + jnp.dot(a_ref[...], w_ref[...], preferred_element_type=F32)


def _out_proj(a, w, x, *, tm):
    n, k = a.shape
    d = w.shape[1]
    assert n % tm == 0
    return pl.pallas_call(
        _out_proj_kernel,
        grid=(n // tm,),
        in_specs=[
            pl.BlockSpec((tm, k), lambda i: (i, 0)),
            pl.BlockSpec((k, d), lambda i: (0, 0)),
            pl.BlockSpec((tm, d), lambda i: (i, 0)),
        ],
        out_specs=pl.BlockSpec((tm, d), lambda i: (i, 0)),
        out_shape=jax.ShapeDtypeStruct((n, d), F32),
        compiler_params=_params("arbitrary"),
        name="out_proj",
    )(a, w, x)


def _gelu_tanh(x):
    return 0.5 * x * (1.0 + jnp.tanh(0.7978845608028654 * (x + 0.044715 * (x * x * x))))


def _softplus(y):
    return jnp.maximum(y, 0.0) + jnp.log1p(jnp.exp(-jnp.abs(y)))


def _scan_rows(a_ref, b_ref, hin_ref, h_in):
    g = SCAN_GROUP
    groups = a_ref.shape[0] // g

    def phase(k):
        return pl.ds(k, groups, stride=g)

    a_acc, b_acc = a_ref[phase(0), :], b_ref[phase(0), :]
    for k in range(1, g):
        ak = a_ref[phase(k), :]
        b_acc = ak * b_acc + b_ref[phase(k), :]
        a_acc = ak * a_acc
        a_ref[phase(k), :] = a_acc
        b_ref[phase(k), :] = b_acc
    hc = h_in
    for gi in range(groups):
        hin_ref[gi:gi + 1, :] = hc
        hc = a_acc[gi:gi + 1, :] * hc + b_acc[gi:gi + 1, :]
    hin = hin_ref[...]
    for k in range(g):
        b_ref[phase(k), :] = a_ref[phase(k), :] * hin + b_ref[phase(k), :]
    return hc


def _rglru_kernel(gate_ref, u_ref, conv0_ref, h0_ref, cw_ref, cb_ref, gw_ref, gb_ref, lam_ref,
                  hg_ref, nconv_ref, nh_ref, ubuf, a_s, b_s, hin_s, hcar, *, tt):
    c = u_ref.shape[-1]
    pad = CONV_PAD_ROWS

    @pl.when(pl.program_id(1) == 0)
    def _():
        ubuf[0:pad, :] = conv0_ref[0]
        hcar[...] = h0_ref[0]

    ubuf[pad:pad + tt, :] = u_ref[0]

    for blk in range(c // RNN_BLOCK):
        cols = slice(blk * RNN_BLOCK, (blk + 1) * RNN_BLOCK)
        uc = cb_ref[:, cols]
        for k in range(CONV_W):
            uc = uc + ubuf[pad - (CONV_W - 1 - k):pad - (CONV_W - 1 - k) + tt, cols] * cw_ref[k:k + 1, cols]
        ub = uc.astype(BF16)
        r = jax.nn.sigmoid(jnp.dot(ub, gw_ref[0, blk], preferred_element_type=F32) + gb_ref[0:1, cols])
        i = jax.nn.sigmoid(jnp.dot(ub, gw_ref[1, blk], preferred_element_type=F32) + gb_ref[1:2, cols])
        log_a = (-LRU_C * r) * _softplus(-lam_ref[:, cols])
        a = jnp.exp(log_a)
        inp = jnp.sqrt(1.0 - a * a) * (i * uc)
        for j in range(RNN_BLOCK // V7X_LANES):
            slab = blk * (RNN_BLOCK // V7X_LANES) + j
            lanes = slice(slab * V7X_LANES, (slab + 1) * V7X_LANES)
            a_s[slab] = a[:, j * V7X_LANES:(j + 1) * V7X_LANES]
            b_s[slab] = inp[:, j * V7X_LANES:(j + 1) * V7X_LANES]
            hcar[:, lanes] = _scan_rows(a_s.at[slab], b_s.at[slab], hin_s.at[slab], hcar[:, lanes])
            hg_ref[0, :, lanes] = (b_s[slab] * _gelu_tanh(gate_ref[0, :, lanes])).astype(BF16)

    tail = ubuf[pad + tt - (CONV_W - 1):pad + tt, :]
    nconv_ref[0] = tail
    ubuf[pad - (CONV_W - 1):pad, :] = tail
    nh_ref[0] = hcar[...]


def _rglru(gate, u, conv0, h0, cw, cb, gw, gb, lam, *, tt):
    b, t, c = u.shape
    assert t % tt == 0 and tt % V7X_SUBLANES == 0 and tt >= CONV_W - 1
    nblk = c // RNN_BLOCK
    conv0_pad = jnp.pad(conv0, ((0, 0), (CONV_PAD_ROWS - (CONV_W - 1), 0), (0, 0)))
    return pl.pallas_call(
        functools.partial(_rglru_kernel, tt=tt),
        grid=(b, t // tt),
        in_specs=[
            pl.BlockSpec((1, tt, c), lambda bi, ti: (bi, ti, 0)),
            pl.BlockSpec((1, tt, c), lambda bi, ti: (bi, ti, 0)),
            pl.BlockSpec((1, CONV_PAD_ROWS, c), lambda bi, ti: (bi, 0, 0)),
            pl.BlockSpec((1, 1, c), lambda bi, ti: (bi, 0, 0)),
            pl.BlockSpec((CONV_W, c), lambda bi, ti: (0, 0)),
            pl.BlockSpec((1, c), lambda bi, ti: (0, 0)),
            pl.BlockSpec((2, nblk, RNN_BLOCK, RNN_BLOCK), lambda bi, ti: (0, 0, 0, 0)),
            pl.BlockSpec((2, c), lambda bi, ti: (0, 0)),
            pl.BlockSpec((1, c), lambda bi, ti: (0, 0)),
        ],
        out_specs=[
            pl.BlockSpec((1, tt, c), lambda bi, ti: (bi, ti, 0)),
            pl.BlockSpec((1, CONV_W - 1, c), lambda bi, ti: (bi, 0, 0)),
            pl.BlockSpec((1, 1, c), lambda bi, ti: (bi, 0, 0)),
        ],
        out_shape=[
            jax.ShapeDtypeStruct((b, t, c), BF16),
            jax.ShapeDtypeStruct((b, CONV_W - 1, c), F32),
            jax.ShapeDtypeStruct((b, 1, c), F32),
        ],
        scratch_shapes=[
            pltpu.VMEM((CONV_PAD_ROWS + tt, c), F32),
            pltpu.VMEM((c // V7X_LANES, tt, V7X_LANES), F32),
            pltpu.VMEM((c // V7X_LANES, tt, V7X_LANES), F32),
            pltpu.VMEM((c // V7X_LANES, tt // SCAN_GROUP, V7X_LANES), F32),
            pltpu.VMEM((1, c), F32),
        ],
        compiler_params=_params("arbitrary", "arbitrary"),
        name="rglru",
    )(gate, u, conv0_pad, h0.reshape(b, 1, c), cw, cb.reshape(1, c), gw, gb, lam.reshape(1, c))


def _cumsum_kernel(x_ref, o_ref, *, rows):
    s = x_ref.shape[1]
    r = lax.broadcasted_iota(jnp.int32, (rows, rows), 0)
    q = lax.broadcasted_iota(jnp.int32, (rows, rows), 1)
    tril = (q <= r).astype(F32)

    def block(bi, carry):
        r0 = pl.multiple_of(bi * rows, rows)
        cs = jnp.dot(tril, x_ref[0, pl.ds(r0, rows), :], preferred_element_type=F32,
                     precision=lax.Precision.HIGHEST) + carry
        o_ref[0, pl.ds(r0, rows), :] = cs
        return cs[rows - 1:rows, :]

    lax.fori_loop(0, s // rows, block, jnp.zeros((1, x_ref.shape[2]), F32))


def _cumsum(x, *, rows):
    b, s, w = x.shape
    assert s % rows == 0
    return pl.pallas_call(
        functools.partial(_cumsum_kernel, rows=rows),
        grid=(b,),
        in_specs=[pl.BlockSpec((1, s, w), lambda bi: (bi, 0, 0))],
        out_specs=pl.BlockSpec((1, s, w), lambda bi: (bi, 0, 0)),
        out_shape=jax.ShapeDtypeStruct((b, s, w), F32),
        compiler_params=_params("arbitrary"),
        name="cumsum",
    )(x)


def _split3(c):
    hi = c.astype(BF16)
    r1 = c - hi.astype(F32)
    mid = r1.astype(BF16)
    lo = (r1 - mid.astype(F32)).astype(BF16)
    return hi, mid, lo


def _aug_lanes(c, head, piece_lanes, one_lanes, sign):
    w = c.shape[1]
    src = lax.broadcasted_iota(jnp.int32, (w, V7X_LANES), 0)
    dst = lax.broadcasted_iota(jnp.int32, (w, V7X_LANES), 1)
    out = None
    for piece, lane in zip(_split3(c), piece_lanes):
        route = jnp.where((src == head) & (dst == lane), sign, 0.0).astype(BF16)
        term = jnp.dot(piece, route, preferred_element_type=F32)
        out = term if out is None else out + term
    lane_id = lax.broadcasted_iota(jnp.int32, out.shape, 1)
    ones = (lane_id >= one_lanes[0]) & (lane_id <= one_lanes[-1])
    return jnp.where(ones, 1.0, out)


def _attn_prompt_kernel(q_ref, k_ref, v_ref, g_ref, c_ref, o_ref,
                        qa_ref, ka_ref, vt_ref, m_ref, l_ref, acc_ref, *, tile):
    head = pl.program_id(1)
    qi = pl.program_id(2)
    n_kt = vt_ref.shape[0]
    dims = (((1,), (1,)), ((), ()))

    @pl.when(qi == 0)
    def _():
        def build(bi, carry):
            rows = pl.ds(pl.multiple_of(bi * tile, tile), tile)
            ka_ref[rows, 0:HEAD_DIM] = k_ref[0, rows, :].astype(BF16)
            ka_ref[rows, HEAD_DIM:] = _aug_lanes(c_ref[0, rows, :], head, AUG_K_LANES, AUG_Q_LANES,
                                                 -1.0).astype(BF16)
            vt_ref[bi] = v_ref[0, rows, :].T.astype(BF16)
            return carry
        lax.fori_loop(0, n_kt, build, 0)

    q0 = pl.multiple_of(qi * (2 * tile), 2 * tile)
    qa_ref[:, 0:HEAD_DIM] = q_ref[0]
    qa_ref[:, HEAD_DIM:] = _aug_lanes(c_ref[0, pl.ds(q0, 2 * tile), :], head, AUG_Q_LANES, AUG_K_LANES,
                                      1.0).astype(BF16)
    m_ref[...] = jnp.full(m_ref.shape, MASK_VALUE, F32)
    l_ref[...] = jnp.zeros(l_ref.shape, F32)
    acc_ref[...] = jnp.zeros(acc_ref.shape, F32)

    def scores(half, kj, masked):
        k0 = pl.multiple_of(kj * tile, tile)
        st = lax.dot_general(ka_ref[pl.ds(k0, tile), :], qa_ref[half * tile:(half + 1) * tile, :], dims,
                             preferred_element_type=F32)
        if masked:
            key = lax.broadcasted_iota(jnp.int32, (tile, tile), 0)
            qry = lax.broadcasted_iota(jnp.int32, (tile, tile), 1)
            st = jnp.where(key <= qry, st, MASK_VALUE)
        return st

    def update(half, kj, st):
        m_prev = m_ref[half]
        m_new = jnp.maximum(m_prev, jnp.max(st, axis=0, keepdims=True))
        alpha = jnp.exp(m_prev - m_new)
        p = jnp.exp(st - m_new)
        l_ref[half] = alpha * l_ref[half] + jnp.sum(p, axis=0, keepdims=True)
        acc_ref[half] = alpha * acc_ref[half] + jnp.dot(vt_ref[kj], p.astype(BF16),
                                                        preferred_element_type=F32)
        m_ref[half] = m_new

    def both(kj, carry):
        st0 = scores(0, kj, False)
        st1 = scores(1, kj, False)
        update(0, kj, st0)
        update(1, kj, st1)
        return carry

    lax.fori_loop(0, 2 * qi, both, 0)
    st0 = scores(0, 2 * qi, True)
    st1 = scores(1, 2 * qi, False)
    update(0, 2 * qi, st0)
    update(1, 2 * qi, st1)
    update(1, 2 * qi + 1, scores(1, 2 * qi + 1, True))
    for half in range(2):
        rows = slice(half * tile, (half + 1) * tile)
        o = (acc_ref[half] / l_ref[half]).T
        o_ref[0, rows, :] = (o * jax.nn.sigmoid(g_ref[0, rows, :])).astype(BF16)


def _attn_prompt(q, k, v, g, c, *, tile):
    b, s, d = q.shape
    nh = d // HEAD_DIM
    assert s % (2 * tile) == 0 and tile % V7X_LANES == 0
    qspec = pl.BlockSpec((1, 2 * tile, HEAD_DIM), lambda bi, hi, qi: (bi, qi, hi))
    kspec = pl.BlockSpec((1, s, HEAD_DIM), lambda bi, hi, qi: (bi, 0, hi))
    return pl.pallas_call(
        functools.partial(_attn_prompt_kernel, tile=tile),
        grid=(b, nh, s // (2 * tile)),
        in_specs=[qspec, kspec, kspec, qspec,
                  pl.BlockSpec((1, s, V7X_LANES), lambda bi, hi, qi: (bi, 0, 0))],
        out_specs=qspec,
        out_shape=jax.ShapeDtypeStruct((b, s, d), BF16),
        scratch_shapes=[
            pltpu.VMEM((2 * tile, 2 * HEAD_DIM), BF16),
            pltpu.VMEM((s, 2 * HEAD_DIM), BF16),
            pltpu.VMEM((s // tile, HEAD_DIM, tile), BF16),
            pltpu.VMEM((2, 1, tile), F32),
            pltpu.VMEM((2, 1, tile), F32),
            pltpu.VMEM((2, HEAD_DIM, tile), F32),
        ],
        compiler_params=_params("arbitrary", "arbitrary", "arbitrary"),
        name="attn_prompt",
    )(q, k, v, g, c)


def _attn_sample_kernel(q_ref, kc_ref, vc_ref, kn_ref, vn_ref, g_ref, c_ref, o_ref):
    t = q_ref.shape[1]
    past = kc_ref.shape[1]
    nh = q_ref.shape[2] // HEAD_DIM
    dims = (((1,), (1,)), ((), ()))
    row = lax.broadcasted_iota(jnp.int32, (t, t), 0)
    col = lax.broadcasted_iota(jnp.int32, (t, t), 1)
    c_all = c_ref[0]
    c_past, c_new = c_all[0:past], c_all[past:past + t]
    for head in range(nh):
        cols = slice(head * HEAD_DIM, (head + 1) * HEAD_DIM)
        qa = jnp.concatenate([q_ref[0, :, cols],
                              _aug_lanes(c_new, head, AUG_Q_LANES, AUG_K_LANES, 1.0).astype(BF16)], axis=1)
        kca = jnp.concatenate([kc_ref[0, :, head, :].astype(BF16),
                               _aug_lanes(c_past, head, AUG_K_LANES, AUG_Q_LANES, -1.0).astype(BF16)], axis=1)
        kna = jnp.concatenate([kn_ref[0, :, cols].astype(BF16),
                               _aug_lanes(c_new, head, AUG_K_LANES, AUG_Q_LANES, -1.0).astype(BF16)], axis=1)
        s_past = lax.dot_general(qa, kca, dims, preferred_element_type=F32)
        s_new = jnp.where(col <= row, lax.dot_general(qa, kna, dims, preferred_element_type=F32), MASK_VALUE)
        m = jnp.maximum(jnp.max(s_past, axis=1, keepdims=True), jnp.max(s_new, axis=1, keepdims=True))
        p_past = jnp.exp(s_past - m)
        p_new = jnp.exp(s_new - m)
        l = jnp.sum(p_past, axis=1, keepdims=True) + jnp.sum(p_new, axis=1, keepdims=True)
        o = (jnp.dot(p_past.astype(BF16), vc_ref[0, :, head, :].astype(BF16), preferred_element_type=F32)
             + jnp.dot(p_new.astype(BF16), vn_ref[0, :, cols].astype(BF16), preferred_element_type=F32)) / l
        o_ref[0, :, cols] = (o * jax.nn.sigmoid(g_ref[0, :, cols])).astype(BF16)


def _attn_sample(q, kc, vc, kn, vn, g, c):
    b, t, d = q.shape
    _, past, nh, hd = kc.shape
    new = pl.BlockSpec((1, t, d), lambda bi: (bi, 0, 0))
    old = pl.BlockSpec((1, past, nh, hd), lambda bi: (bi, 0, 0, 0))
    return pl.pallas_call(
        _attn_sample_kernel,
        grid=(b,),
        in_specs=[new, old, old, new, new, new,
                  pl.BlockSpec((1, past + t, V7X_LANES), lambda bi: (bi, 0, 0))],
        out_specs=new,
        out_shape=jax.ShapeDtypeStruct((b, t, d), BF16),
        compiler_params=_params("arbitrary"),
        name="attn_sample",
    )(q, kc, vc, kn, vn, g, c)


FFN_TM, FFN_TF = 1024, 512
NORM_ROW_CHUNK, FFN_COL_CHUNK = 256, 512
PROJ_TM, PROJ_TN_RNN, PROJ_TN_ATTN = 1024, 640, 512
OUT_TM = 512
RGLRU_TT_PROMPT = 256
CUMSUM_ROWS_PROMPT, CUMSUM_ROWS_SAMPLE = 256, 64
ATTN_TILE = 512


def kernel(x_prompt, x_sample, state_conv, state_h, cache_k, cache_v, cache_logf, ffn_norm, ffn_w_in,
           ffn_w_out, a_norm, a_w_in, a_conv_w, a_conv_b, a_gate_w, a_gate_b, a_lambda, a_w_out, kv_norm,
           w_kv, w_f, b_f, b_norm, b_w_qg, b_w_o, final_norm):
    bp, sp, d = x_prompt.shape
    bs, ts, _ = x_sample.shape
    depth = ffn_w_in.shape[0]
    n_a = a_w_in.shape[0]
    n_heads = w_f.shape[1]
    past = cache_k.shape[1]

    ffn_w_in_b, ffn_w_out_b = ffn_w_in.astype(BF16), ffn_w_out.astype(BF16)
    a_w_in_b, a_gate_w_b, a_w_out_b = a_w_in.astype(BF16), a_gate_w.astype(BF16), a_w_out.astype(BF16)
    w_kv_b, b_w_qg_b, b_w_o_b = w_kv.astype(BF16), b_w_qg.astype(BF16), b_w_o.astype(BF16)
    wf_pad = jnp.pad(w_f, ((0, 0), (0, V7X_LANES - n_heads))).astype(BF16)
    bf_pad = jnp.pad(b_f, (0, V7X_LANES - n_heads)).reshape(1, V7X_LANES)

    streams = {"p": x_prompt.reshape(bp * sp, d), "s": x_sample.reshape(bs * ts, d)}
    batch = {"p": (bp, sp), "s": (bs, ts)}
    conv_out = {"p": [], "s": []}
    h_out = {"p": [], "s": []}
    kv = {}

    def ffn(x, l, half, apply_final=False):
        return _ffn(x, ffn_norm[l, half], ffn_w_in_b, ffn_w_out_b, final_norm, layer=l, half=half,
                    apply_final=apply_final, tm=FFN_TM, tf=FFN_TF)

    for l in range(depth):
        for name in ("p", "s"):
            x = ffn(streams[name], l, 0)
            b, t = batch[name]
            if l < n_a:
                gate, u = _proj2(x, a_norm[l], a_w_in_b[l], tm=PROJ_TM, tn=PROJ_TN_RNN)
                c_rnn = u.shape[1]
                if name == "p":
                    conv0 = jnp.zeros((b, CONV_W - 1, c_rnn), F32)
                    h0 = jnp.zeros((b, c_rnn), F32)
                    tt = RGLRU_TT_PROMPT
                else:
                    conv0, h0, tt = state_conv[l], state_h[l], t
                hg, nconv, nh = _rglru(gate.reshape(b, t, c_rnn), u.reshape(b, t, c_rnn), conv0, h0,
                                       a_conv_w[l], a_conv_b[l], a_gate_w_b[l], a_gate_b[l], a_lambda[l], tt=tt)
                conv_out[name].append(nconv)
                h_out[name].append(nh.reshape(b, c_rnn))
                x = _out_proj(hg.reshape(b * t, c_rnn), a_w_out_b[l], x, tm=OUT_TM)
            else:
                jb = l - n_a
                q, g = _proj2(x, b_norm[jb], b_w_qg_b[jb], tm=PROJ_TM, tn=PROJ_TN_ATTN, a_scale=ATTN_SCALE,
                              a_dtype=BF16)
                k, v, f_pad = kv[name]
                if name == "p":
                    c = _cumsum(f_pad.reshape(b, t, V7X_LANES), rows=CUMSUM_ROWS_PROMPT)
                    og = _attn_prompt(q.reshape(b, t, d), k.reshape(b, t, d), v.reshape(b, t, d),
                                      g.reshape(b, t, d), c, tile=ATTN_TILE)
                else:
                    cache_f_pad = jnp.pad(cache_logf, ((0, 0), (0, 0), (0, V7X_LANES - n_heads)))
                    f_all = jnp.concatenate([cache_f_pad, f_pad.reshape(b, t, V7X_LANES)], axis=1)
                    c = _cumsum(f_all, rows=CUMSUM_ROWS_SAMPLE)
                    og = _attn_sample(q.reshape(b, t, d), cache_k, cache_v, k.reshape(b, t, d),
                                      v.reshape(b, t, d), g.reshape(b, t, d), c)
                x = _out_proj(og.reshape(b * t, d), b_w_o_b[jb], x, tm=OUT_TM)
            x = ffn(x, l, 1, apply_final=(l == depth - 1))
            if l == n_a - 1:
                kv[name] = _kv_proj(x, kv_norm, w_kv_b, wf_pad, bf_pad, tm=PROJ_TM, tn=PROJ_TN_ATTN)
            streams[name] = x

    def finish(name):
        b, t = batch[name]
        k, v, f_pad = kv[name]
        return (streams[name].reshape(b, t, d), jnp.stack(conv_out[name]), jnp.stack(h_out[name]),
                k.reshape(b, t, n_heads, HEAD_DIM), v.reshape(b, t, n_heads, HEAD_DIM),
                f_pad[:, :n_heads].reshape(b, t, n_heads))

    yp, convp, hp, kp, vp, fp = finish("p")
    ys, convs, hs, ks, vs, fs = finish("s")
    return (yp, ys, convp, hp, kp, vp, fp, convs, hs, ks, vs, fs)
```

```python
import functools

import jax
import jax.numpy as jnp
from jax import lax
from jax.experimental import pallas as pl
from jax.experimental.pallas import tpu as pltpu

F32 = jnp.float32
BF16 = jnp.bfloat16

RMS_EPS = 1e-6
LRU_C = 8.0
HEAD_DIM = 128
ATTN_SCALE = HEAD_DIM ** -0.5
LOG2E = 1.4426950408889634
MASK_VALUE = -1e30

V7X_LANES = 128
V7X_SUBLANES = 8
V7X_VMEM_BYTES = 64 * 1024 * 1024
VMEM_LIMIT_BYTES = V7X_VMEM_BYTES - 8 * 1024 * 1024

RNN_BLOCK = 256
SCAN_GROUP = 8
CONV_W = 4
CONV_PAD_ROWS = V7X_SUBLANES

AUG_Q_LANES = (0, 1, 2)
AUG_K_LANES = (3, 4, 5)
ONES_ROWS = 16


def _params(*semantics):
    return pltpu.CompilerParams(dimension_semantics=semantics, vmem_limit_bytes=VMEM_LIMIT_BYTES)


def _rms(x, g):
    y = x * lax.rsqrt(jnp.mean(x * x, axis=-1, keepdims=True) + RMS_EPS)
    return y * g


def _ffn_kernel(x_ref, g_ref, wg_ref, wu_ref, wo_ref, fin_ref, o_ref, xn_ref, *, apply_final, row_chunk,
                col_chunk):
    j = pl.program_id(1)
    tm, d = x_ref.shape

    def by_rows(fn):
        def body(ri, carry):
            fn(pl.ds(pl.multiple_of(ri * row_chunk, row_chunk), row_chunk))
            return carry
        lax.fori_loop(0, tm // row_chunk, body, 0)

    @pl.when(j == 0)
    def _():
        def prep(rows):
            x = x_ref[rows, :]
            xn_ref[rows, :] = _rms(x, g_ref[...]).astype(BF16)
            o_ref[rows, :] = x
        by_rows(prep)

    xn = xn_ref[...]
    gate = jnp.dot(xn, wg_ref[...], preferred_element_type=F32)
    up = jnp.dot(xn, wu_ref[...], preferred_element_type=F32)
    h = (0.5 * (gate * jax.nn.sigmoid(gate)) * up).astype(BF16)
    for c in range(d // col_chunk):
        cols = slice(c * col_chunk, (c + 1) * col_chunk)
        o_ref[:, cols] += jnp.dot(h, wo_ref[:, cols], preferred_element_type=F32)

    if apply_final:
        @pl.when(j == pl.num_programs(1) - 1)
        def _():
            def norm(rows):
                o_ref[rows, :] = _rms(o_ref[rows, :], fin_ref[...])
            by_rows(norm)


def _ffn(x, g, w_in, w_out, fin, *, layer, half, apply_final, tm, tf):
    n, d = x.shape
    dff = w_out.shape[2]
    nj = dff // tf
    assert n % tm == 0 and dff % tf == 0 and w_in.shape[2:] == (d, 2 * dff)
    row_chunk, col_chunk = min(NORM_ROW_CHUNK, tm), min(FFN_COL_CHUNK, d)
    assert tm % row_chunk == 0 and d % col_chunk == 0
    return pl.pallas_call(
        functools.partial(_ffn_kernel, apply_final=apply_final, row_chunk=row_chunk, col_chunk=col_chunk),
        grid=(n // tm, nj),
        in_specs=[
            pl.BlockSpec((tm, d), lambda i, j: (i, 0)),
            pl.BlockSpec((1, d), lambda i, j: (0, 0)),
            pl.BlockSpec((None, None, d, tf), lambda i, j: (layer, half, 0, j)),
            pl.BlockSpec((None, None, d, tf), lambda i, j: (layer, half, 0, j + nj)),
            pl.BlockSpec((None, None, tf, d), lambda i, j: (layer, half, j, 0)),
            pl.BlockSpec((1, d), lambda i, j: (0, 0)),
        ],
        out_specs=pl.BlockSpec((tm, d), lambda i, j: (i, 0)),
        out_shape=jax.ShapeDtypeStruct((n, d), F32),
        scratch_shapes=[pltpu.VMEM((tm, d), BF16)],
        compiler_params=_params("arbitrary", "arbitrary"),
        name="ffn",
    )(x, g.reshape(1, d), w_in, w_in, w_out, fin.reshape(1, d))


def _gelu_tanh(x):
    return 0.5 * x * (1.0 + jnp.tanh(0.7978845608028654 * (x + 0.044715 * (x * x * x))))


_EPILOGUES = {
    None: lambda y: y,
    "gelu": _gelu_tanh,
    "sigmoid": jax.nn.sigmoid,
    "attn_scale": lambda y: y * (ATTN_SCALE * LOG2E),
}


def _proj2_kernel(x_ref, g_ref, w_ref, oa_ref, ob_ref, *, a_epilogue, b_epilogue):
    half = w_ref.shape[1] // 2
    xn = _rms(x_ref[...], g_ref[...]).astype(BF16)
    a = jnp.dot(xn, w_ref[:, 0:half], preferred_element_type=F32)
    oa_ref[...] = _EPILOGUES[a_epilogue](a).astype(oa_ref.dtype)
    b = jnp.dot(xn, w_ref[:, half:], preferred_element_type=F32)
    ob_ref[...] = _EPILOGUES[b_epilogue](b).astype(ob_ref.dtype)


def _proj2(x, g, w, *, tm, a_epilogue=None, b_epilogue=None, a_dtype=F32):
    n, d = x.shape
    half = w.shape[1] // 2
    assert n % tm == 0
    return pl.pallas_call(
        functools.partial(_proj2_kernel, a_epilogue=a_epilogue, b_epilogue=b_epilogue),
        grid=(n // tm,),
        in_specs=[
            pl.BlockSpec((tm, d), lambda i: (i, 0)),
            pl.BlockSpec((1, d), lambda i: (0, 0)),
            pl.BlockSpec(w.shape, lambda i: (0, 0), pipeline_mode=pl.Buffered(1)),
        ],
        out_specs=[pl.BlockSpec((tm, half), lambda i: (i, 0))] * 2,
        out_shape=[jax.ShapeDtypeStruct((n, half), a_dtype), jax.ShapeDtypeStruct((n, half), F32)],
        compiler_params=_params("arbitrary"),
        name="proj2",
    )(x, g.reshape(1, d), w)


def _log_sigmoid(z):
    return jnp.minimum(z, 0.0) - jnp.log1p(jnp.exp(-jnp.abs(z)))


def _kv_kernel(x_ref, g_ref, w_ref, wf_ref, bf_ref, k_ref, v_ref, f_ref):
    half = w_ref.shape[1] // 2
    xn = _rms(x_ref[...], g_ref[...]).astype(BF16)
    k_ref[...] = jnp.dot(xn, w_ref[:, 0:half], preferred_element_type=F32)
    v_ref[...] = jnp.dot(xn, w_ref[:, half:], preferred_element_type=F32)
    z = jnp.dot(xn, wf_ref[...], preferred_element_type=F32) + bf_ref[...]
    f_ref[...] = _log_sigmoid(z)


def _kv_proj(x, g, w_kv, wf_pad, bf_pad, *, tm):
    n, d = x.shape
    half = w_kv.shape[1] // 2
    assert n % tm == 0
    out = jax.ShapeDtypeStruct((n, half), F32)
    return pl.pallas_call(
        _kv_kernel,
        grid=(n // tm,),
        in_specs=[
            pl.BlockSpec((tm, d), lambda i: (i, 0)),
            pl.BlockSpec((1, d), lambda i: (0, 0)),
            pl.BlockSpec(w_kv.shape, lambda i: (0, 0), pipeline_mode=pl.Buffered(1)),
            pl.BlockSpec((d, V7X_LANES), lambda i: (0, 0)),
            pl.BlockSpec((1, V7X_LANES), lambda i: (0, 0)),
        ],
        out_specs=[
            pl.BlockSpec((tm, half), lambda i: (i, 0)),
            pl.BlockSpec((tm, half), lambda i: (i, 0)),
            pl.BlockSpec((tm, V7X_LANES), lambda i: (i, 0)),
        ],
        out_shape=[out, out, jax.ShapeDtypeStruct((n, V7X_LANES), F32)],
        compiler_params=_params("arbitrary"),
        name="kv_proj",
    )(x, g.reshape(1, d), w_kv, wf_pad, bf_pad)


def _out_proj_kernel(a_ref, w_ref, x_ref, o_ref):
    o_ref[...] = x_ref[...] + jnp.dot(a_ref[...], w_ref[...], preferred_element_type=F32)


def _out_proj(a, w, x, *, tm):
    n, k = a.shape
    d = w.shape[1]
    assert n % tm == 0
    return pl.pallas_call(
        _out_proj_kernel,
        grid=(n // tm,),
        in_specs=[
            pl.BlockSpec((tm, k), lambda i: (i, 0)),
            pl.BlockSpec((k, d), lambda i: (0, 0)),
            pl.BlockSpec((tm, d), lambda i: (i, 0)),
        ],
        out_specs=pl.BlockSpec((tm, d), lambda i: (i, 0)),
        out_shape=jax.ShapeDtypeStruct((n, d), F32),
        compiler_params=_params("arbitrary"),
        name="out_proj",
    )(a, w, x)


def _softplus(y):
    return jnp.maximum(y, 0.0) + jnp.log1p(jnp.exp(-jnp.abs(y)))


def _scan_rows(a_ref, b_ref, hin_ref, h_in):
    g = SCAN_GROUP
    groups = a_ref.shape[0] // g

    def phase(k):
        return pl.ds(k, groups, stride=g)

    a_acc, b_acc = a_ref[phase(0), :], b_ref[phase(0), :]
    for k in range(1, g):
        ak = a_ref[phase(k), :]
        b_acc = ak * b_acc + b_ref[phase(k), :]
        a_acc = ak * a_acc
        a_ref[phase(k), :] = a_acc
        b_ref[phase(k), :] = b_acc
    hc = h_in
    for gi in range(groups):
        hin_ref[gi:gi + 1, :] = hc
        hc = a_acc[gi:gi + 1, :] * hc + b_acc[gi:gi + 1, :]
    hin = hin_ref[...]
    for k in range(g):
        b_ref[phase(k), :] = a_ref[phase(k), :] * hin + b_ref[phase(k), :]
    return hc


def _rglru_kernel(gate_ref, u_ref, conv0_ref, h0_ref, cw_ref, cb_ref, gw_ref, gb_ref, lam_ref,
                  hg_ref, nconv_ref, nh_ref, ubuf, a_s, b_s, hin_s, hcar, *, tt):
    c = u_ref.shape[-1]
    pad = CONV_PAD_ROWS

    @pl.when(pl.program_id(1) == 0)
    def _():
        ubuf[0:pad, :] = conv0_ref[0]
        hcar[...] = h0_ref[0]

    ubuf[pad:pad + tt, :] = u_ref[0]

    for blk in range(c // RNN_BLOCK):
        cols = slice(blk * RNN_BLOCK, (blk + 1) * RNN_BLOCK)
        uc = cb_ref[:, cols]
        for k in range(CONV_W):
            uc = uc + ubuf[pad - (CONV_W - 1 - k):pad - (CONV_W - 1 - k) + tt, cols] * cw_ref[k:k + 1, cols]
        ub = uc.astype(BF16)
        r = jax.nn.sigmoid(jnp.dot(ub, gw_ref[0, blk], preferred_element_type=F32) + gb_ref[0:1, cols])
        i = jax.nn.sigmoid(jnp.dot(ub, gw_ref[1, blk], preferred_element_type=F32) + gb_ref[1:2, cols])
        log_a = (-LRU_C * r) * _softplus(-lam_ref[:, cols])
        a = jnp.exp(log_a)
        inp = jnp.sqrt(1.0 - a * a) * (i * uc)
        for j in range(RNN_BLOCK // V7X_LANES):
            slab = blk * (RNN_BLOCK // V7X_LANES) + j
            lanes = slice(slab * V7X_LANES, (slab + 1) * V7X_LANES)
            a_s[slab] = a[:, j * V7X_LANES:(j + 1) * V7X_LANES]
            b_s[slab] = inp[:, j * V7X_LANES:(j + 1) * V7X_LANES]
            hcar[:, lanes] = _scan_rows(a_s.at[slab], b_s.at[slab], hin_s.at[slab], hcar[:, lanes])
            hg_ref[0, :, lanes] = (b_s[slab] * gate_ref[0, :, lanes]).astype(BF16)

    tail = ubuf[pad + tt - (CONV_W - 1):pad + tt, :]
    nconv_ref[0] = tail
    ubuf[pad - (CONV_W - 1):pad, :] = tail
    nh_ref[0] = hcar[...]


def _rglru(gate, u, conv0, h0, cw, cb, gw, gb, lam, *, tt):
    b, t, c = u.shape
    assert t % tt == 0 and tt % V7X_SUBLANES == 0 and tt >= CONV_W - 1
    nblk = c // RNN_BLOCK
    conv0_pad = jnp.pad(conv0, ((0, 0), (CONV_PAD_ROWS - (CONV_W - 1), 0), (0, 0)))
    return pl.pallas_call(
        functools.partial(_rglru_kernel, tt=tt),
        grid=(b, t // tt),
        in_specs=[
            pl.BlockSpec((1, tt, c), lambda bi, ti: (bi, ti, 0)),
            pl.BlockSpec((1, tt, c), lambda bi, ti: (bi, ti, 0)),
            pl.BlockSpec((1, CONV_PAD_ROWS, c), lambda bi, ti: (bi, 0, 0)),
            pl.BlockSpec((1, 1, c), lambda bi, ti: (bi, 0, 0)),
            pl.BlockSpec((CONV_W, c), lambda bi, ti: (0, 0)),
            pl.BlockSpec((1, c), lambda bi, ti: (0, 0)),
            pl.BlockSpec((2, nblk, RNN_BLOCK, RNN_BLOCK), lambda bi, ti: (0, 0, 0, 0)),
            pl.BlockSpec((2, c), lambda bi, ti: (0, 0)),
            pl.BlockSpec((1, c), lambda bi, ti: (0, 0)),
        ],
        out_specs=[
            pl.BlockSpec((1, tt, c), lambda bi, ti: (bi, ti, 0)),
            pl.BlockSpec((1, CONV_W - 1, c), lambda bi, ti: (bi, 0, 0)),
            pl.BlockSpec((1, 1, c), lambda bi, ti: (bi, 0, 0)),
        ],
        out_shape=[
            jax.ShapeDtypeStruct((b, t, c), BF16),
            jax.ShapeDtypeStruct((b, CONV_W - 1, c), F32),
            jax.ShapeDtypeStruct((b, 1, c), F32),
        ],
        scratch_shapes=[
            pltpu.VMEM((CONV_PAD_ROWS + tt, c), F32),
            pltpu.VMEM((c // V7X_LANES, tt, V7X_LANES), F32),
            pltpu.VMEM((c // V7X_LANES, tt, V7X_LANES), F32),
            pltpu.VMEM((c // V7X_LANES, tt // SCAN_GROUP, V7X_LANES), F32),
            pltpu.VMEM((1, c), F32),
        ],
        compiler_params=_params("arbitrary", "arbitrary"),
        name="rglru",
    )(gate, u, conv0_pad, h0.reshape(b, 1, c), cw, cb.reshape(1, c), gw, gb, lam.reshape(1, c))


def _cumsum_kernel(x_ref, o_ref, *, rows):
    s = x_ref.shape[1]
    r = lax.broadcasted_iota(jnp.int32, (rows, rows), 0)
    q = lax.broadcasted_iota(jnp.int32, (rows, rows), 1)
    tril = (q <= r).astype(F32)

    def block(bi, carry):
        r0 = pl.multiple_of(bi * rows, rows)
        cs = jnp.dot(tril, x_ref[0, pl.ds(r0, rows), :], preferred_element_type=F32,
                     precision=lax.Precision.HIGHEST) + carry
        o_ref[0, pl.ds(r0, rows), :] = cs
        return cs[rows - 1:rows, :]

    lax.fori_loop(0, s // rows, block, jnp.zeros((1, x_ref.shape[2]), F32))


def _cumsum(x, *, rows):
    b, s, w = x.shape
    assert s % rows == 0
    return pl.pallas_call(
        functools.partial(_cumsum_kernel, rows=rows),
        grid=(b,),
        in_specs=[pl.BlockSpec((1, s, w), lambda bi: (bi, 0, 0))],
        out_specs=pl.BlockSpec((1, s, w), lambda bi: (bi, 0, 0)),
        out_shape=jax.ShapeDtypeStruct((b, s, w), F32),
        compiler_params=_params("arbitrary"),
        name="cumsum",
    )(x)


def _split3(c):
    hi = c.astype(BF16)
    r1 = c - hi.astype(F32)
    mid = r1.astype(BF16)
    lo = (r1 - mid.astype(F32)).astype(BF16)
    return hi, mid, lo


def _aug_lanes(c, head, piece_lanes, one_lanes, sign):
    w = c.shape[1]
    src = lax.broadcasted_iota(jnp.int32, (w, V7X_LANES), 0)
    dst = lax.broadcasted_iota(jnp.int32, (w, V7X_LANES), 1)
    out = None
    for piece, lane in zip(_split3(c * LOG2E), piece_lanes):
        route = jnp.where((src == head) & (dst == lane), sign, 0.0).astype(BF16)
        term = jnp.dot(piece, route, preferred_element_type=F32)
        out = term if out is None else out + term
    lane_id = lax.broadcasted_iota(jnp.int32, out.shape, 1)
    ones = (lane_id >= one_lanes[0]) & (lane_id <= one_lanes[-1])
    return jnp.where(ones, 1.0, out)


def _attn_prompt_kernel(q_ref, k_ref, v_ref, g_ref, c_ref, o_ref,
                        qa_ref, ka_ref, vt_ref, st_ref, mt_ref, m_ref, acc_ref, *, tile):
    head = pl.program_id(1)
    qi = pl.program_id(2)
    n_kt = vt_ref.shape[0]
    dims = (((1,), (1,)), ((), ()))

    @pl.when(qi == 0)
    def _():
        def build(bi, carry):
            rows = pl.ds(pl.multiple_of(bi * tile, tile), tile)
            ka_ref[rows, 0:HEAD_DIM] = k_ref[0, rows, :].astype(BF16)
            ka_ref[rows, HEAD_DIM:] = _aug_lanes(c_ref[0, rows, :], head, AUG_K_LANES, AUG_Q_LANES,
                                                 -1.0).astype(BF16)
            vt_ref[bi, 0:HEAD_DIM, :] = v_ref[0, rows, :].T.astype(BF16)
            vt_ref[bi, HEAD_DIM:, :] = jnp.ones((ONES_ROWS, tile), BF16)
            return carry
        lax.fori_loop(0, n_kt, build, 0)

    q0 = pl.multiple_of(qi * (2 * tile), 2 * tile)
    qa_ref[:, 0:HEAD_DIM] = q_ref[0]
    qa_ref[:, HEAD_DIM:] = _aug_lanes(c_ref[0, pl.ds(q0, 2 * tile), :], head, AUG_Q_LANES, AUG_K_LANES,
                                      1.0).astype(BF16)
    m_ref[...] = jnp.full(m_ref.shape, MASK_VALUE, F32)
    acc_ref[...] = jnp.zeros(acc_ref.shape, F32)

    def scores(half, kj, slot):
        k0 = pl.multiple_of(kj * tile, tile)
        st = lax.dot_general(ka_ref[pl.ds(k0, tile), :], qa_ref[half * tile:(half + 1) * tile, :], dims,
                             preferred_element_type=F32)
        st_ref[slot] = st
        mt_ref[slot] = jnp.max(st, axis=0, keepdims=True)

    def update(half, kj, slot, masked):
        st = st_ref[slot]
        if masked:
            key = lax.broadcasted_iota(jnp.int32, (tile, tile), 0)
            qry = lax.broadcasted_iota(jnp.int32, (tile, tile), 1)
            st = jnp.where(key <= qry, st, MASK_VALUE)
            m_tile = jnp.max(st, axis=0, keepdims=True)
        else:
            m_tile = mt_ref[slot]
        m_prev = m_ref[half]
        m_new = jnp.maximum(m_prev, m_tile)
        alpha = jnp.exp2(m_prev - m_new)
        p = jnp.exp2(st - m_new).astype(BF16)
        acc_ref[half] = alpha * acc_ref[half] + jnp.dot(vt_ref[kj], p, preferred_element_type=F32)
        m_ref[half] = m_new

    scores(0, 0, 0)

    def full_tiles(kj, carry):
        scores(1, kj, 1)
        update(0, kj, 0, False)
        scores(0, kj + 1, 0)
        update(1, kj, 1, False)
        return carry

    lax.fori_loop(0, 2 * qi, full_tiles, 0)
    scores(1, 2 * qi, 1)
    update(0, 2 * qi, 0, True)
    scores(1, 2 * qi + 1, 0)
    update(1, 2 * qi, 1, False)
    update(1, 2 * qi + 1, 0, True)
    for half in range(2):
        rows = slice(half * tile, (half + 1) * tile)
        o = (acc_ref[half, 0:HEAD_DIM, :] / acc_ref[half, HEAD_DIM:HEAD_DIM + 1, :]).T
        o_ref[0, rows, :] = (o * g_ref[0, rows, :]).astype(BF16)


def _attn_prompt(q, k, v, g, c, *, tile):
    b, s, d = q.shape
    nh = d // HEAD_DIM
    assert s % (2 * tile) == 0 and tile % V7X_LANES == 0
    qspec = pl.BlockSpec((1, 2 * tile, HEAD_DIM), lambda bi, hi, qi: (bi, qi, hi))
    kspec = pl.BlockSpec((1, s, HEAD_DIM), lambda bi, hi, qi: (bi, 0, hi))
    return pl.pallas_call(
        functools.partial(_attn_prompt_kernel, tile=tile),
        grid=(b, nh, s // (2 * tile)),
        in_specs=[qspec, kspec, kspec, qspec,
                  pl.BlockSpec((1, s, V7X_LANES), lambda bi, hi, qi: (bi, 0, 0))],
        out_specs=qspec,
        out_shape=jax.ShapeDtypeStruct((b, s, d), BF16),
        scratch_shapes=[
            pltpu.VMEM((2 * tile, 2 * HEAD_DIM), BF16),
            pltpu.VMEM((s, 2 * HEAD_DIM), BF16),
            pltpu.VMEM((s // tile, HEAD_DIM + ONES_ROWS, tile), BF16),
            pltpu.VMEM((2, tile, tile), F32),
            pltpu.VMEM((2, 1, tile), F32),
            pltpu.VMEM((2, 1, tile), F32),
            pltpu.VMEM((2, HEAD_DIM + ONES_ROWS, tile), F32),
        ],
        compiler_params=_params("arbitrary", "arbitrary", "arbitrary"),
        name="attn_prompt",
    )(q, k, v, g, c)


def _attn_sample_kernel(q_ref, kc_ref, vc_ref, kn_ref, vn_ref, g_ref, c_ref, o_ref):
    t = q_ref.shape[1]
    past = kc_ref.shape[1]
    nh = q_ref.shape[2] // HEAD_DIM
    dims = (((1,), (1,)), ((), ()))
    row = lax.broadcasted_iota(jnp.int32, (t, t), 0)
    col = lax.broadcasted_iota(jnp.int32, (t, t), 1)
    c_all = c_ref[0]
    c_past, c_new = c_all[0:past], c_all[past:past + t]
    for head in range(nh):
        cols = slice(head * HEAD_DIM, (head + 1) * HEAD_DIM)
        qa = jnp.concatenate([q_ref[0, :, cols],
                              _aug_lanes(c_new, head, AUG_Q_LANES, AUG_K_LANES, 1.0).astype(BF16)], axis=1)
        kca = jnp.concatenate([kc_ref[0, :, head, :].astype(BF16),
                               _aug_lanes(c_past, head, AUG_K_LANES, AUG_Q_LANES, -1.0).astype(BF16)], axis=1)
        kna = jnp.concatenate([kn_ref[0, :, cols].astype(BF16),
                               _aug_lanes(c_new, head, AUG_K_LANES, AUG_Q_LANES, -1.0).astype(BF16)], axis=1)
        s_past = lax.dot_general(qa, kca, dims, preferred_element_type=F32)
        s_new = jnp.where(col <= row, lax.dot_general(qa, kna, dims, preferred_element_type=F32), MASK_VALUE)
        m = jnp.maximum(jnp.max(s_past, axis=1, keepdims=True), jnp.max(s_new, axis=1, keepdims=True))
        p_past = jnp.exp2(s_past - m)
        p_new = jnp.exp2(s_new - m)
        l = jnp.sum(p_past, axis=1, keepdims=True) + jnp.sum(p_new, axis=1, keepdims=True)
        o = (jnp.dot(p_past.astype(BF16), vc_ref[0, :, head, :].astype(BF16), preferred_element_type=F32)
             + jnp.dot(p_new.astype(BF16), vn_ref[0, :, cols].astype(BF16), preferred_element_type=F32)) / l
        o_ref[0, :, cols] = (o * g_ref[0, :, cols]).astype(BF16)


def _attn_sample(q, kc, vc, kn, vn, g, c):
    b, t, d = q.shape
    _, past, nh, hd = kc.shape
    new = pl.BlockSpec((1, t, d), lambda bi: (bi, 0, 0))
    old = pl.BlockSpec((1, past, nh, hd), lambda bi: (bi, 0, 0, 0))
    return pl.pallas_call(
        _attn_sample_kernel,
        grid=(b,),
        in_specs=[new, old, old, new, new, new,
                  pl.BlockSpec((1, past + t, V7X_LANES), lambda bi: (bi, 0, 0))],
        out_specs=new,
        out_shape=jax.ShapeDtypeStruct((b, t, d), BF16),
        compiler_params=_params("arbitrary"),
        name="attn_sample",
    )(q, kc, vc, kn, vn, g, c)


FFN_TM, FFN_TF = 1024, 512
NORM_ROW_CHUNK, FFN_COL_CHUNK = 256, 512
PROJ_TM = 256
OUT_TM = 512
RGLRU_TT_PROMPT = 256
CUMSUM_ROWS_PROMPT, CUMSUM_ROWS_SAMPLE = 256, 64
ATTN_TILE = 512


def kernel(x_prompt, x_sample, state_conv, state_h, cache_k, cache_v, cache_logf, ffn_norm, ffn_w_in,
           ffn_w_out, a_norm, a_w_in, a_conv_w, a_conv_b, a_gate_w, a_gate_b, a_lambda, a_w_out, kv_norm,
           w_kv, w_f, b_f, b_norm, b_w_qg, b_w_o, final_norm):
    bp, sp, d = x_prompt.shape
    bs, ts, _ = x_sample.shape
    depth = ffn_w_in.shape[0]
    n_a = a_w_in.shape[0]
    n_heads = w_f.shape[1]
    past = cache_k.shape[1]

    ffn_w_in_b, ffn_w_out_b = ffn_w_in.astype(BF16), ffn_w_out.astype(BF16)
    a_w_in_b, a_gate_w_b, a_w_out_b = a_w_in.astype(BF16), a_gate_w.astype(BF16), a_w_out.astype(BF16)
    w_kv_b, b_w_qg_b, b_w_o_b = w_kv.astype(BF16), b_w_qg.astype(BF16), b_w_o.astype(BF16)
    wf_pad = jnp.pad(w_f, ((0, 0), (0, V7X_LANES - n_heads))).astype(BF16)
    bf_pad = jnp.pad(b_f, (0, V7X_LANES - n_heads)).reshape(1, V7X_LANES)

    streams = {"p": x_prompt.reshape(bp * sp, d), "s": x_sample.reshape(bs * ts, d)}
    batch = {"p": (bp, sp), "s": (bs, ts)}
    conv_out = {"p": [], "s": []}
    h_out = {"p": [], "s": []}
    kv = {}

    def ffn(x, l, half, apply_final=False):
        return _ffn(x, ffn_norm[l, half], ffn_w_in_b, ffn_w_out_b, final_norm, layer=l, half=half,
                    apply_final=apply_final, tm=FFN_TM, tf=FFN_TF)

    for l in range(depth):
        for name in ("p", "s"):
            x = ffn(streams[name], l, 0)
            b, t = batch[name]
            if l < n_a:
                gate, u = _proj2(x, a_norm[l], a_w_in_b[l], tm=PROJ_TM, a_epilogue="gelu")
                c_rnn = u.shape[1]
                if name == "p":
                    conv0 = jnp.zeros((b, CONV_W - 1, c_rnn), F32)
                    h0 = jnp.zeros((b, c_rnn), F32)
                    tt = RGLRU_TT_PROMPT
                else:
                    conv0, h0, tt = state_conv[l], state_h[l], t
                hg, nconv, nh = _rglru(gate.reshape(b, t, c_rnn), u.reshape(b, t, c_rnn), conv0, h0,
                                       a_conv_w[l], a_conv_b[l], a_gate_w_b[l], a_gate_b[l], a_lambda[l], tt=tt)
                conv_out[name].append(nconv)
                h_out[name].append(nh.reshape(b, c_rnn))
                x = _out_proj(hg.reshape(b * t, c_rnn), a_w_out_b[l], x, tm=OUT_TM)
            else:
                jb = l - n_a
                q, g = _proj2(x, b_norm[jb], b_w_qg_b[jb], tm=PROJ_TM, a_epilogue="attn_scale",
                              b_epilogue="sigmoid", a_dtype=BF16)
                k, v, f_pad = kv[name]
                if name == "p":
                    c = _cumsum(f_pad.reshape(b, t, V7X_LANES), rows=CUMSUM_ROWS_PROMPT)
                    og = _attn_prompt(q.reshape(b, t, d), k.reshape(b, t, d), v.reshape(b, t, d),
                                      g.reshape(b, t, d), c, tile=ATTN_TILE)
                else:
                    cache_f_pad = jnp.pad(cache_logf, ((0, 0), (0, 0), (0, V7X_LANES - n_heads)))
                    f_all = jnp.concatenate([cache_f_pad, f_pad.reshape(b, t, V7X_LANES)], axis=1)
                    c = _cumsum(f_all, rows=CUMSUM_ROWS_SAMPLE)
                    og = _attn_sample(q.reshape(b, t, d), cache_k, cache_v, k.reshape(b, t, d),
                                      v.reshape(b, t, d), g.reshape(b, t, d), c)
                x = _out_proj(og.reshape(b * t, d), b_w_o_b[jb], x, tm=OUT_TM)
            x = ffn(x, l, 1, apply_final=(l == depth - 1))
            if l == n_a - 1:
                kv[name] = _kv_proj(x, kv_norm, w_kv_b, wf_pad, bf_pad, tm=PROJ_TM)
            streams[name] = x

    def finish(name):
        b, t = batch[name]
        k, v, f_pad = kv[name]
        return (streams[name].reshape(b, t, d), jnp.stack(conv_out[name]), jnp.stack(h_out[name]),
                k.reshape(b, t, n_heads, HEAD_DIM), v.reshape(b, t, n_heads, HEAD_DIM),
                f_pad[:, :n_heads].reshape(b, t, n_heads))

    yp, convp, hp, kp, vp, fp = finish("p")
    ys, convs, hs, ks, vs, fs = finish("s")
    return (yp, ys, convp, hp, kp, vp, fp, convs, hs, ks, vs, fs)
```

```python
import functools

import jax
import jax.numpy as jnp
from jax import lax
from jax.experimental import pallas as pl
from jax.experimental.pallas import tpu as pltpu

F32 = jnp.float32
BF16 = jnp.bfloat16

RMS_EPS = 1e-6
LRU_C = 8.0
HEAD_DIM = 128
ATTN_SCALE = HEAD_DIM ** -0.5
LOG2E = 1.4426950408889634
MASK_VALUE = -1e30

V7X_LANES = 128
V7X_SUBLANES = 8
V7X_VMEM_BYTES = 64 * 1024 * 1024
VMEM_LIMIT_BYTES = V7X_VMEM_BYTES - 8 * 1024 * 1024

RNN_BLOCK = 256
SCAN_GROUP = 8
CONV_W = 4
CONV_PAD_ROWS = V7X_SUBLANES

AUG_Q_LANES = (0, 1, 2)
AUG_K_LANES = (3, 4, 5)
ONES_ROWS = 16


def _params(*semantics):
    return pltpu.CompilerParams(dimension_semantics=semantics, vmem_limit_bytes=VMEM_LIMIT_BYTES)


def _rms(x, g):
    y = x * lax.rsqrt(jnp.mean(x * x, axis=-1, keepdims=True) + RMS_EPS)
    return y * g


def _ffn_kernel(x_ref, g_ref, wg_ref, wu_ref, wo_ref, fin_ref, o_ref, xn_ref, *, apply_final, row_chunk,
                col_chunk):
    j = pl.program_id(1)
    tm, d = x_ref.shape

    def by_rows(fn):
        def body(ri, carry):
            fn(pl.ds(pl.multiple_of(ri * row_chunk, row_chunk), row_chunk))
            return carry
        lax.fori_loop(0, tm // row_chunk, body, 0)

    @pl.when(j == 0)
    def _():
        def prep(rows):
            x = x_ref[rows, :]
            xn_ref[rows, :] = _rms(x, g_ref[...]).astype(BF16)
            o_ref[rows, :] = x
        by_rows(prep)

    xn = xn_ref[...]
    gate = jnp.dot(xn, wg_ref[...], preferred_element_type=F32)
    up = jnp.dot(xn, wu_ref[...], preferred_element_type=F32)
    h = (0.5 * (gate * jax.nn.sigmoid(gate)) * up).astype(BF16)
    for c in range(d // col_chunk):
        cols = slice(c * col_chunk, (c + 1) * col_chunk)
        o_ref[:, cols] += jnp.dot(h, wo_ref[:, cols], preferred_element_type=F32)

    if apply_final:
        @pl.when(j == pl.num_programs(1) - 1)
        def _():
            def norm(rows):
                o_ref[rows, :] = _rms(o_ref[rows, :], fin_ref[...])
            by_rows(norm)


def _ffn(x, g, w_in, w_out, fin, *, layer, half, apply_final, tm, tf):
    n, d = x.shape
    dff = w_out.shape[2]
    nj = dff // tf
    assert n % tm == 0 and dff % tf == 0 and w_in.shape[2:] == (d, 2 * dff)
    row_chunk, col_chunk = min(NORM_ROW_CHUNK, tm), min(FFN_COL_CHUNK, d)
    assert tm % row_chunk == 0 and d % col_chunk == 0
    return pl.pallas_call(
        functools.partial(_ffn_kernel, apply_final=apply_final, row_chunk=row_chunk, col_chunk=col_chunk),
        grid=(n // tm, nj),
        in_specs=[
            pl.BlockSpec((tm, d), lambda i, j: (i, 0)),
            pl.BlockSpec((1, d), lambda i, j: (0, 0)),
            pl.BlockSpec((None, None, d, tf), lambda i, j: (layer, half, 0, j)),
            pl.BlockSpec((None, None, d, tf), lambda i, j: (layer, half, 0, j + nj)),
            pl.BlockSpec((None, None, tf, d), lambda i, j: (layer, half, j, 0)),
            pl.BlockSpec((1, d), lambda i, j: (0, 0)),
        ],
        out_specs=pl.BlockSpec((tm, d), lambda i, j: (i, 0)),
        out_shape=jax.ShapeDtypeStruct((n, d), F32),
        scratch_shapes=[pltpu.VMEM((tm, d), BF16)],
        compiler_params=_params("arbitrary", "arbitrary"),
        name="ffn",
    )(x, g.reshape(1, d), w_in, w_in, w_out, fin.reshape(1, d))


def _gelu_tanh(x):
    return 0.5 * x * (1.0 + jnp.tanh(0.7978845608028654 * (x + 0.044715 * (x * x * x))))


_EPILOGUES = {
    None: lambda y: y,
    "gelu": _gelu_tanh,
    "sigmoid": jax.nn.sigmoid,
    "attn_scale": lambda y: y * (ATTN_SCALE * LOG2E),
}


def _proj2_kernel(x_ref, g_ref, w_ref, oa_ref, ob_ref, *, a_epilogue, b_epilogue):
    half = w_ref.shape[1] // 2
    xn = _rms(x_ref[...], g_ref[...]).astype(BF16)
    a = jnp.dot(xn, w_ref[:, 0:half], preferred_element_type=F32)
    oa_ref[...] = _EPILOGUES[a_epilogue](a).astype(oa_ref.dtype)
    b = jnp.dot(xn, w_ref[:, half:], preferred_element_type=F32)
    ob_ref[...] = _EPILOGUES[b_epilogue](b).astype(ob_ref.dtype)


def _proj2(x, g, w, *, tm, a_epilogue=None, b_epilogue=None, a_dtype=F32):
    n, d = x.shape
    half = w.shape[1] // 2
    assert n % tm == 0
    return pl.pallas_call(
        functools.partial(_proj2_kernel, a_epilogue=a_epilogue, b_epilogue=b_epilogue),
        grid=(n // tm,),
        in_specs=[
            pl.BlockSpec((tm, d), lambda i: (i, 0)),
            pl.BlockSpec((1, d), lambda i: (0, 0)),
            pl.BlockSpec(w.shape, lambda i: (0, 0), pipeline_mode=pl.Buffered(1)),
        ],
        out_specs=[pl.BlockSpec((tm, half), lambda i: (i, 0))] * 2,
        out_shape=[jax.ShapeDtypeStruct((n, half), a_dtype), jax.ShapeDtypeStruct((n, half), F32)],
        compiler_params=_params("arbitrary"),
        name="proj2",
    )(x, g.reshape(1, d), w)


def _log_sigmoid(z):
    return jnp.minimum(z, 0.0) - jnp.log1p(jnp.exp(-jnp.abs(z)))


def _kv_kernel(x_ref, g_ref, w_ref, wf_ref, bf_ref, k_ref, v_ref, f_ref):
    half = w_ref.shape[1] // 2
    xn = _rms(x_ref[...], g_ref[...]).astype(BF16)
    k_ref[...] = jnp.dot(xn, w_ref[:, 0:half], preferred_element_type=F32)
    v_ref[...] = jnp.dot(xn, w_ref[:, half:], preferred_element_type=F32)
    z = jnp.dot(xn, wf_ref[...], preferred_element_type=F32) + bf_ref[...]
    f_ref[...] = _log_sigmoid(z)


def _kv_proj(x, g, w_kv, wf_pad, bf_pad, *, tm):
    n, d = x.shape
    half = w_kv.shape[1] // 2
    assert n % tm == 0
    out = jax.ShapeDtypeStruct((n, half), F32)
    return pl.pallas_call(
        _kv_kernel,
        grid=(n // tm,),
        in_specs=[
            pl.BlockSpec((tm, d), lambda i: (i, 0)),
            pl.BlockSpec((1, d), lambda i: (0, 0)),
            pl.BlockSpec(w_kv.shape, lambda i: (0, 0), pipeline_mode=pl.Buffered(1)),
            pl.BlockSpec((d, V7X_LANES), lambda i: (0, 0)),
            pl.BlockSpec((1, V7X_LANES), lambda i: (0, 0)),
        ],
        out_specs=[
            pl.BlockSpec((tm, half), lambda i: (i, 0)),
            pl.BlockSpec((tm, half), lambda i: (i, 0)),
            pl.BlockSpec((tm, V7X_LANES), lambda i: (i, 0)),
        ],
        out_shape=[out, out, jax.ShapeDtypeStruct((n, V7X_LANES), F32)],
        compiler_params=_params("arbitrary"),
        name="kv_proj",
    )(x, g.reshape(1, d), w_kv, wf_pad, bf_pad)


def _out_proj_kernel(a_ref, w_ref, x_ref, o_ref):
    o_ref[...] = x_ref[...] + jnp.dot(a_ref[...], w_ref[...], preferred_element_type=F32)


def _out_proj(a, w, x, *, tm):
    n, k = a.shape
    d = w.shape[1]
    assert n % tm == 0
    return pl.pallas_call(
        _out_proj_kernel,
        grid=(n // tm,),
        in_specs=[
            pl.BlockSpec((tm, k), lambda i: (i, 0)),
            pl.BlockSpec((k, d), lambda i: (0, 0)),
            pl.BlockSpec((tm, d), lambda i: (i, 0)),
        ],
        out_specs=pl.BlockSpec((tm, d), lambda i: (i, 0)),
        out_shape=jax.ShapeDtypeStruct((n, d), F32),
        compiler_params=_params("arbitrary"),
        name="out_proj",
    )(a, w, x)


def _softplus(y):
    return jnp.maximum(y, 0.0) + jnp.log1p(jnp.exp(-jnp.abs(y)))


def _scan_rows(a_ref, b_ref, hin_ref, h_in):
    g = SCAN_GROUP
    groups = a_ref.shape[0] // g

    def phase(k):
        return pl.ds(k, groups, stride=g)

    a_acc, b_acc = a_ref[phase(0), :], b_ref[phase(0), :]
    for k in range(1, g):
        ak = a_ref[phase(k), :]
        b_acc = ak * b_acc + b_ref[phase(k), :]
        a_acc = ak * a_acc
        a_ref[phase(k), :] = a_acc
        b_ref[phase(k), :] = b_acc
    hc = h_in
    for gi in range(groups):
        hin_ref[gi:gi + 1, :] = hc
        hc = a_acc[gi:gi + 1, :] * hc + b_acc[gi:gi + 1, :]
    hin = hin_ref[...]
    for k in range(g):
        b_ref[phase(k), :] = a_ref[phase(k), :] * hin + b_ref[phase(k), :]
    return hc


def _rglru_kernel(gate_ref, u_ref, conv0_ref, h0_ref, cw_ref, cb_ref, gw_ref, gb_ref, lam_ref,
                  hg_ref, nconv_ref, nh_ref, ubuf, a_s, b_s, hin_s, hcar, *, tt):
    c = u_ref.shape[-1]
    pad = CONV_PAD_ROWS

    @pl.when(pl.program_id(1) == 0)
    def _():
        ubuf[0:pad, :] = conv0_ref[0]
        hcar[...] = h0_ref[0]

    ubuf[pad:pad + tt, :] = u_ref[0]

    for blk in range(c // RNN_BLOCK):
        cols = slice(blk * RNN_BLOCK, (blk + 1) * RNN_BLOCK)
        uc = cb_ref[:, cols]
        for k in range(CONV_W):
            uc = uc + ubuf[pad - (CONV_W - 1 - k):pad - (CONV_W - 1 - k) + tt, cols] * cw_ref[k:k + 1, cols]
        ub = uc.astype(BF16)
        r = jax.nn.sigmoid(jnp.dot(ub, gw_ref[0, blk], preferred_element_type=F32) + gb_ref[0:1, cols])
        i = jax.nn.sigmoid(jnp.dot(ub, gw_ref[1, blk], preferred_element_type=F32) + gb_ref[1:2, cols])
        log_a = (-LRU_C * r) * _softplus(-lam_ref[:, cols])
        a = jnp.exp(log_a)
        inp = jnp.sqrt(1.0 - a * a) * (i * uc)
        for j in range(RNN_BLOCK // V7X_LANES):
            slab = blk * (RNN_BLOCK // V7X_LANES) + j
            lanes = slice(slab * V7X_LANES, (slab + 1) * V7X_LANES)
            a_s[slab] = a[:, j * V7X_LANES:(j + 1) * V7X_LANES]
            b_s[slab] = inp[:, j * V7X_LANES:(j + 1) * V7X_LANES]
            hcar[:, lanes] = _scan_rows(a_s.at[slab], b_s.at[slab], hin_s.at[slab], hcar[:, lanes])
            hg_ref[0, :, lanes] = (b_s[slab] * gate_ref[0, :, lanes]).astype(BF16)

    tail = ubuf[pad + tt - (CONV_W - 1):pad + tt, :]
    nconv_ref[0] = tail
    ubuf[pad - (CONV_W - 1):pad, :] = tail
    nh_ref[0] = hcar[...]


def _rglru(gate, u, conv0, h0, cw, cb, gw, gb, lam, *, tt):
    b, t, c = u.shape
    assert t % tt == 0 and tt % V7X_SUBLANES == 0 and tt >= CONV_W - 1
    nblk = c // RNN_BLOCK
    conv0_pad = jnp.pad(conv0, ((0, 0), (CONV_PAD_ROWS - (CONV_W - 1), 0), (0, 0)))
    return pl.pallas_call(
        functools.partial(_rglru_kernel, tt=tt),
        grid=(b, t // tt),
        in_specs=[
            pl.BlockSpec((1, tt, c), lambda bi, ti: (bi, ti, 0)),
            pl.BlockSpec((1, tt, c), lambda bi, ti: (bi, ti, 0)),
            pl.BlockSpec((1, CONV_PAD_ROWS, c), lambda bi, ti: (bi, 0, 0)),
            pl.BlockSpec((1, 1, c), lambda bi, ti: (bi, 0, 0)),
            pl.BlockSpec((CONV_W, c), lambda bi, ti: (0, 0)),
            pl.BlockSpec((1, c), lambda bi, ti: (0, 0)),
            pl.BlockSpec((2, nblk, RNN_BLOCK, RNN_BLOCK), lambda bi, ti: (0, 0, 0, 0)),
            pl.BlockSpec((2, c), lambda bi, ti: (0, 0)),
            pl.BlockSpec((1, c), lambda bi, ti: (0, 0)),
        ],
        out_specs=[
            pl.BlockSpec((1, tt, c), lambda bi, ti: (bi, ti, 0)),
            pl.BlockSpec((1, CONV_W - 1, c), lambda bi, ti: (bi, 0, 0)),
            pl.BlockSpec((1, 1, c), lambda bi, ti: (bi, 0, 0)),
        ],
        out_shape=[
            jax.ShapeDtypeStruct((b, t, c), BF16),
            jax.ShapeDtypeStruct((b, CONV_W - 1, c), F32),
            jax.ShapeDtypeStruct((b, 1, c), F32),
        ],
        scratch_shapes=[
            pltpu.VMEM((CONV_PAD_ROWS + tt, c), F32),
            pltpu.VMEM((c // V7X_LANES, tt, V7X_LANES), F32),
            pltpu.VMEM((c // V7X_LANES, tt, V7X_LANES), F32),
            pltpu.VMEM((c // V7X_LANES, tt // SCAN_GROUP, V7X_LANES), F32),
            pltpu.VMEM((1, c), F32),
        ],
        compiler_params=_params("arbitrary", "arbitrary"),
        name="rglru",
    )(gate, u, conv0_pad, h0.reshape(b, 1, c), cw, cb.reshape(1, c), gw, gb, lam.reshape(1, c))


def _cumsum_kernel(x_ref, o_ref, *, rows):
    s = x_ref.shape[1]
    r = lax.broadcasted_iota(jnp.int32, (rows, rows), 0)
    q = lax.broadcasted_iota(jnp.int32, (rows, rows), 1)
    tril = (q <= r).astype(F32)

    def block(bi, carry):
        r0 = pl.multiple_of(bi * rows, rows)
        cs = jnp.dot(tril, x_ref[0, pl.ds(r0, rows), :], preferred_element_type=F32,
                     precision=lax.Precision.HIGHEST) + carry
        o_ref[0, pl.ds(r0, rows), :] = cs
        return cs[rows - 1:rows, :]

    lax.fori_loop(0, s // rows, block, jnp.zeros((1, x_ref.shape[2]), F32))


def _cumsum(x, *, rows):
    b, s, w = x.shape
    assert s % rows == 0
    return pl.pallas_call(
        functools.partial(_cumsum_kernel, rows=rows),
        grid=(b,),
        in_specs=[pl.BlockSpec((1, s, w), lambda bi: (bi, 0, 0))],
        out_specs=pl.BlockSpec((1, s, w), lambda bi: (bi, 0, 0)),
        out_shape=jax.ShapeDtypeStruct((b, s, w), F32),
        compiler_params=_params("arbitrary"),
        name="cumsum",
    )(x)


def _split3(c):
    hi = c.astype(BF16)
    r1 = c - hi.astype(F32)
    mid = r1.astype(BF16)
    lo = (r1 - mid.astype(F32)).astype(BF16)
    return hi, mid, lo


def _aug_pieces(c, head):
    w = c.shape[1]
    src = lax.broadcasted_iota(jnp.int32, (w, V7X_LANES), 0)
    dst = lax.broadcasted_iota(jnp.int32, (w, V7X_LANES), 1)
    out = None
    for piece, q_lane, k_lane in zip(_split3(c * LOG2E), AUG_Q_LANES, AUG_K_LANES):
        sign = jnp.where(dst == q_lane, 1.0, jnp.where(dst == k_lane, -1.0, 0.0))
        route = jnp.where(src == head, sign, 0.0).astype(BF16)
        term = jnp.dot(piece, route, preferred_element_type=F32)
        out = term if out is None else out + term
    return out


def _aug_side(pieces, one_lanes):
    lane_id = lax.broadcasted_iota(jnp.int32, pieces.shape, 1)
    ones = (lane_id >= one_lanes[0]) & (lane_id <= one_lanes[-1])
    return jnp.where(ones, 1.0, pieces).astype(BF16)


def _attn_prompt_kernel(q_ref, k_ref, v_ref, g_ref, c_ref, o_ref,
                        qa_ref, ka_ref, qx_ref, vt_ref, st_ref, mt_ref, m_ref, acc_ref, *, tile):
    head = pl.program_id(1)
    qi = pl.program_id(2)
    n_kt = vt_ref.shape[0]
    dims = (((1,), (1,)), ((), ()))

    @pl.when(qi == 0)
    def _():
        def build(bi, carry):
            rows = pl.ds(pl.multiple_of(bi * tile, tile), tile)
            ka_ref[rows, 0:HEAD_DIM] = k_ref[0, rows, :].astype(BF16)
            pieces = _aug_pieces(c_ref[0, rows, :], head)
            ka_ref[rows, HEAD_DIM:] = _aug_side(pieces, AUG_Q_LANES)
            qx_ref[rows, :] = _aug_side(pieces, AUG_K_LANES)
            vt_ref[bi, 0:HEAD_DIM, :] = v_ref[0, rows, :].T.astype(BF16)
            vt_ref[bi, HEAD_DIM:, :] = jnp.ones((ONES_ROWS, tile), BF16)
            return carry
        lax.fori_loop(0, n_kt, build, 0)

    q0 = pl.multiple_of(qi * (2 * tile), 2 * tile)
    qa_ref[:, 0:HEAD_DIM] = q_ref[0]
    qa_ref[:, HEAD_DIM:] = qx_ref[pl.ds(q0, 2 * tile), :]
    m_ref[...] = jnp.full(m_ref.shape, MASK_VALUE, F32)
    acc_ref[...] = jnp.zeros(acc_ref.shape, F32)

    def scores(half, kj, slot):
        k0 = pl.multiple_of(kj * tile, tile)
        st = lax.dot_general(ka_ref[pl.ds(k0, tile), :], qa_ref[half * tile:(half + 1) * tile, :], dims,
                             preferred_element_type=F32)
        st_ref[slot] = st
        mt_ref[slot] = jnp.max(st, axis=0, keepdims=True)

    def update(half, kj, slot, masked):
        st = st_ref[slot]
        if masked:
            key = lax.broadcasted_iota(jnp.int32, (tile, tile), 0)
            qry = lax.broadcasted_iota(jnp.int32, (tile, tile), 1)
            st = jnp.where(key <= qry, st, MASK_VALUE)
            m_tile = jnp.max(st, axis=0, keepdims=True)
        else:
            m_tile = mt_ref[slot]
        m_prev = m_ref[half]
        m_new = jnp.maximum(m_prev, m_tile)
        alpha = jnp.exp2(m_prev - m_new)
        p = jnp.exp2(st - m_new).astype(BF16)
        acc_ref[half] = alpha * acc_ref[half] + jnp.dot(vt_ref[kj], p, preferred_element_type=F32)
        m_ref[half] = m_new

    scores(0, 0, 0)

    def full_tiles(kj, carry):
        scores(1, kj, 1)
        update(0, kj, 0, False)
        scores(0, kj + 1, 0)
        update(1, kj, 1, False)
        return carry

    lax.fori_loop(0, 2 * qi, full_tiles, 0)
    scores(1, 2 * qi, 1)
    update(0, 2 * qi, 0, True)
    scores(1, 2 * qi + 1, 0)
    update(1, 2 * qi, 1, False)
    update(1, 2 * qi + 1, 0, True)
    for half in range(2):
        rows = slice(half * tile, (half + 1) * tile)
        o = (acc_ref[half, 0:HEAD_DIM, :] / acc_ref[half, HEAD_DIM:HEAD_DIM + 1, :]).T
        o_ref[0, rows, :] = (o * g_ref[0, rows, :]).astype(BF16)


def _attn_prompt(q, k, v, g, c, *, tile):
    b, s, d = q.shape
    nh = d // HEAD_DIM
    assert s % (2 * tile) == 0 and tile % V7X_LANES == 0
    qspec = pl.BlockSpec((1, 2 * tile, HEAD_DIM), lambda bi, hi, qi: (bi, qi, hi))
    kspec = pl.BlockSpec((1, s, HEAD_DIM), lambda bi, hi, qi: (bi, 0, hi))
    return pl.pallas_call(
        functools.partial(_attn_prompt_kernel, tile=tile),
        grid=(b, nh, s // (2 * tile)),
        in_specs=[qspec, kspec, kspec, qspec,
                  pl.BlockSpec((1, s, V7X_LANES), lambda bi, hi, qi: (bi, 0, 0))],
        out_specs=qspec,
        out_shape=jax.ShapeDtypeStruct((b, s, d), BF16),
        scratch_shapes=[
            pltpu.VMEM((2 * tile, 2 * HEAD_DIM), BF16),
            pltpu.VMEM((s, 2 * HEAD_DIM), BF16),
            pltpu.VMEM((s, HEAD_DIM), BF16),
            pltpu.VMEM((s // tile, HEAD_DIM + ONES_ROWS, tile), BF16),
            pltpu.VMEM((2, tile, tile), F32),
            pltpu.VMEM((2, 1, tile), F32),
            pltpu.VMEM((2, 1, tile), F32),
            pltpu.VMEM((2, HEAD_DIM + ONES_ROWS, tile), F32),
        ],
        compiler_params=_params("arbitrary", "arbitrary", "arbitrary"),
        name="attn_prompt",
    )(q, k, v, g, c)


def _attn_sample_kernel(q_ref, kc_ref, vc_ref, kn_ref, vn_ref, g_ref, c_ref, o_ref):
    t = q_ref.shape[1]
    nh = q_ref.shape[2] // HEAD_DIM
    past = kc_ref.shape[1] // nh
    dims = (((1,), (1,)), ((), ()))
    row = lax.broadcasted_iota(jnp.int32, (t, t), 0)
    col = lax.broadcasted_iota(jnp.int32, (t, t), 1)
    c_all = c_ref[0]
    c_past, c_new = c_all[0:past], c_all[past:past + t]
    def scores(head):
        cols = slice(head * HEAD_DIM, (head + 1) * HEAD_DIM)
        new_pieces = _aug_pieces(c_new, head)
        qa = jnp.concatenate([q_ref[0, :, cols], _aug_side(new_pieces, AUG_K_LANES)], axis=1)
        kca = jnp.concatenate([kc_ref[0, pl.ds(head, past, stride=nh), :].astype(BF16),
                               _aug_side(_aug_pieces(c_past, head), AUG_Q_LANES)], axis=1)
        kna = jnp.concatenate([kn_ref[0, :, cols].astype(BF16), _aug_side(new_pieces, AUG_Q_LANES)], axis=1)
        s_past = lax.dot_general(qa, kca, dims, preferred_element_type=F32)
        s_new = jnp.where(col <= row, lax.dot_general(qa, kna, dims, preferred_element_type=F32), MASK_VALUE)
        return s_past, s_new

    def finish(head, s_past, s_new):
        cols = slice(head * HEAD_DIM, (head + 1) * HEAD_DIM)
        m = jnp.maximum(jnp.max(s_past, axis=1, keepdims=True), jnp.max(s_new, axis=1, keepdims=True))
        p_past = jnp.exp2(s_past - m)
        p_new = jnp.exp2(s_new - m)
        l = jnp.sum(p_past, axis=1, keepdims=True) + jnp.sum(p_new, axis=1, keepdims=True)
        v_past = vc_ref[0, pl.ds(head, past, stride=nh), :].astype(BF16)
        o = (jnp.dot(p_past.astype(BF16), v_past, preferred_element_type=F32)
             + jnp.dot(p_new.astype(BF16), vn_ref[0, :, cols].astype(BF16), preferred_element_type=F32)) / l
        o_ref[0, :, cols] = (o * g_ref[0, :, cols]).astype(BF16)

    pending = scores(0)
    for head in range(nh):
        current, pending = pending, (scores(head + 1) if head + 1 < nh else None)
        finish(head, *current)


def _attn_sample(q, kc, vc, kn, vn, g, c):
    b, t, d = q.shape
    _, past, nh, hd = kc.shape
    assert nh % V7X_SUBLANES == 0
    kc, vc = kc.reshape(b, past * nh, hd), vc.reshape(b, past * nh, hd)
    new = pl.BlockSpec((1, t, d), lambda bi: (bi, 0, 0))
    old = pl.BlockSpec((1, past * nh, hd), lambda bi: (bi, 0, 0))
    return pl.pallas_call(
        _attn_sample_kernel,
        grid=(b,),
        in_specs=[new, old, old, new, new, new,
                  pl.BlockSpec((1, past + t, V7X_LANES), lambda bi: (bi, 0, 0))],
        out_specs=new,
        out_shape=jax.ShapeDtypeStruct((b, t, d), BF16),
        compiler_params=_params("arbitrary"),
        name="attn_sample",
    )(q, kc, vc, kn, vn, g, c)


FFN_TM, FFN_TF = 1024, 512
NORM_ROW_CHUNK, FFN_COL_CHUNK = 256, 512
PROJ_TM = 256
OUT_TM = 512
RGLRU_TT_PROMPT = 256
CUMSUM_ROWS_PROMPT, CUMSUM_ROWS_SAMPLE = 256, 272
ATTN_TILE = 512


def kernel(x_prompt, x_sample, state_conv, state_h, cache_k, cache_v, cache_logf, ffn_norm, ffn_w_in,
           ffn_w_out, a_norm, a_w_in, a_conv_w, a_conv_b, a_gate_w, a_gate_b, a_lambda, a_w_out, kv_norm,
           w_kv, w_f, b_f, b_norm, b_w_qg, b_w_o, final_norm):
    bp, sp, d = x_prompt.shape
    bs, ts, _ = x_sample.shape
    depth = ffn_w_in.shape[0]
    n_a = a_w_in.shape[0]
    n_heads = w_f.shape[1]
    past = cache_k.shape[1]

    ffn_w_in_b, ffn_w_out_b = ffn_w_in.astype(BF16), ffn_w_out.astype(BF16)
    a_w_in_b, a_gate_w_b, a_w_out_b = a_w_in.astype(BF16), a_gate_w.astype(BF16), a_w_out.astype(BF16)
    w_kv_b, b_w_qg_b, b_w_o_b = w_kv.astype(BF16), b_w_qg.astype(BF16), b_w_o.astype(BF16)
    wf_pad = jnp.pad(w_f, ((0, 0), (0, V7X_LANES - n_heads))).astype(BF16)
    bf_pad = jnp.pad(b_f, (0, V7X_LANES - n_heads)).reshape(1, V7X_LANES)

    streams = {"p": x_prompt.reshape(bp * sp, d), "s": x_sample.reshape(bs * ts, d)}
    batch = {"p": (bp, sp), "s": (bs, ts)}
    conv_out = {"p": [], "s": []}
    h_out = {"p": [], "s": []}
    kv = {}

    def ffn(x, l, half, apply_final=False):
        return _ffn(x, ffn_norm[l, half], ffn_w_in_b, ffn_w_out_b, final_norm, layer=l, half=half,
                    apply_final=apply_final, tm=FFN_TM, tf=FFN_TF)

    for l in range(depth):
        for name in ("p", "s"):
            x = ffn(streams[name], l, 0)
            b, t = batch[name]
            if l < n_a:
                gate, u = _proj2(x, a_norm[l], a_w_in_b[l], tm=PROJ_TM, a_epilogue="gelu")
                c_rnn = u.shape[1]
                if name == "p":
                    conv0 = jnp.zeros((b, CONV_W - 1, c_rnn), F32)
                    h0 = jnp.zeros((b, c_rnn), F32)
                    tt = RGLRU_TT_PROMPT
                else:
                    conv0, h0, tt = state_conv[l], state_h[l], t
                hg, nconv, nh = _rglru(gate.reshape(b, t, c_rnn), u.reshape(b, t, c_rnn), conv0, h0,
                                       a_conv_w[l], a_conv_b[l], a_gate_w_b[l], a_gate_b[l], a_lambda[l], tt=tt)
                conv_out[name].append(nconv)
                h_out[name].append(nh.reshape(b, c_rnn))
                x = _out_proj(hg.reshape(b * t, c_rnn), a_w_out_b[l], x, tm=OUT_TM)
            else:
                jb = l - n_a
                q, g = _proj2(x, b_norm[jb], b_w_qg_b[jb], tm=PROJ_TM, a_epilogue="attn_scale",
                              b_epilogue="sigmoid", a_dtype=BF16)
                k, v, f_pad = kv[name]
                if name == "p":
                    c = _cumsum(f_pad.reshape(b, t, V7X_LANES), rows=CUMSUM_ROWS_PROMPT)
                    og = _attn_prompt(q.reshape(b, t, d), k.reshape(b, t, d), v.reshape(b, t, d),
                                      g.reshape(b, t, d), c, tile=ATTN_TILE)
                else:
                    cache_f_pad = jnp.pad(cache_logf, ((0, 0), (0, 0), (0, V7X_LANES - n_heads)))
                    f_all = jnp.concatenate([cache_f_pad, f_pad.reshape(b, t, V7X_LANES)], axis=1)
                    c = _cumsum(f_all, rows=CUMSUM_ROWS_SAMPLE)
                    og = _attn_sample(q.reshape(b, t, d), cache_k, cache_v, k.reshape(b, t, d),
                                      v.reshape(b, t, d), g.reshape(b, t, d), c)
                x = _out_proj(og.reshape(b * t, d), b_w_o_b[jb], x, tm=OUT_TM)
            x = ffn(x, l, 1, apply_final=(l == depth - 1))
            if l == n_a - 1:
                kv[name] = _kv_proj(x, kv_norm, w_kv_b, wf_pad, bf_pad, tm=PROJ_TM)
            streams[name] = x

    def finish(name):
        b, t = batch[name]
        k, v, f_pad = kv[name]
        return (streams[name].reshape(b, t, d), jnp.stack(conv_out[name]), jnp.stack(h_out[name]),
                k.reshape(b, t, n_heads, HEAD_DIM), v.reshape(b, t, n_heads, HEAD_DIM),
                f_pad[:, :n_heads].reshape(b, t, n_heads))

    yp, convp, hp, kp, vp, fp = finish("p")
    ys, convs, hs, ks, vs, fs = finish("s")
    return (yp, ys, convp, hp, kp, vp, fp, convs, hs, ks, vs, fs)
```

```python
import functools

import jax
import jax.numpy as jnp
from jax import lax
from jax.experimental import pallas as pl
from jax.experimental.pallas import tpu as pltpu

F32 = jnp.float32
BF16 = jnp.bfloat16

RMS_EPS = 1e-6
LRU_C = 8.0
HEAD_DIM = 128
ATTN_SCALE = HEAD_DIM ** -0.5
LOG2E = 1.4426950408889634
MASK_VALUE = -1e30

V7X_LANES = 128
V7X_SUBLANES = 8
V7X_VMEM_BYTES = 64 * 1024 * 1024
VMEM_LIMIT_BYTES = V7X_VMEM_BYTES - 5 * 1024 * 1024

RNN_BLOCK = 256
SCAN_GROUP = 8
CONV_W = 4
CONV_PAD_ROWS = V7X_SUBLANES

AUG_Q_LANES = (0, 1, 2)
AUG_K_LANES = (3, 4, 5)
ONES_ROWS = 16


def _params(*semantics):
    return pltpu.CompilerParams(dimension_semantics=semantics, vmem_limit_bytes=VMEM_LIMIT_BYTES)


def _rms(x, g):
    y = x * lax.rsqrt(jnp.mean(x * x, axis=-1, keepdims=True) + RMS_EPS)
    return y * g


def _ffn_kernel(x_ref, g_ref, wg_ref, wu_ref, wo_ref, fin_ref, *rest, apply_final, cast_next, row_chunk,
                col_chunk):
    if cast_next:
        src_in_ref, src_out_ref, o_ref, dst_in_ref, dst_out_ref, xn_ref = rest
        dst_in_ref[...] = src_in_ref[...].astype(BF16)
        dst_out_ref[...] = src_out_ref[...].astype(BF16)
    else:
        o_ref, xn_ref = rest
    j = pl.program_id(1)
    tm, d = x_ref.shape

    def by_rows(fn):
        def body(ri, carry):
            fn(pl.ds(pl.multiple_of(ri * row_chunk, row_chunk), row_chunk))
            return carry
        lax.fori_loop(0, tm // row_chunk, body, 0)

    @pl.when(j == 0)
    def _():
        def prep(rows):
            x = x_ref[rows, :]
            xn_ref[rows, :] = _rms(x, g_ref[...]).astype(BF16)
            o_ref[rows, :] = x
        by_rows(prep)

    xn = xn_ref[...]
    gate = jnp.dot(xn, wg_ref[...], preferred_element_type=F32)
    up = jnp.dot(xn, wu_ref[...], preferred_element_type=F32)
    h = (0.5 * (gate * jax.nn.sigmoid(gate)) * up).astype(BF16)
    for c in range(d // col_chunk):
        cols = slice(c * col_chunk, (c + 1) * col_chunk)
        o_ref[:, cols] += jnp.dot(h, wo_ref[:, cols], preferred_element_type=F32)

    if apply_final:
        @pl.when(j == pl.num_programs(1) - 1)
        def _():
            def norm(rows):
                o_ref[rows, :] = _rms(o_ref[rows, :], fin_ref[...])
            by_rows(norm)


def _ffn(x, g, w_in, w_out, fin, *, apply_final, tm, tf, cast_next=None):
    n, d = x.shape
    dff = w_out.shape[0]
    ni, nj = n // tm, dff // tf
    assert n % tm == 0 and dff % tf == 0 and w_in.shape == (d, 2 * dff)
    row_chunk, col_chunk = min(NORM_ROW_CHUNK, tm), min(FFN_COL_CHUNK, d)
    assert tm % row_chunk == 0 and d % col_chunk == 0
    in_specs = [
        pl.BlockSpec((tm, d), lambda i, j: (i, 0)),
        pl.BlockSpec((1, d), lambda i, j: (0, 0)),
        pl.BlockSpec((d, tf), lambda i, j: (0, j)),
        pl.BlockSpec((d, tf), lambda i, j: (0, j + nj)),
        pl.BlockSpec((tf, d), lambda i, j: (j, 0)),
        pl.BlockSpec((1, d), lambda i, j: (0, 0)),
    ]
    out_specs = [pl.BlockSpec((tm, d), lambda i, j: (i, 0))]
    out_shape = [jax.ShapeDtypeStruct((n, d), F32)]
    operands = [x, g.reshape(1, d), w_in, w_in, w_out, fin.reshape(1, d)]
    if cast_next is not None:
        src_in, src_out, layer, half = cast_next
        bi, bo = (d // ni, 2 * dff // nj), (dff // nj, d // ni)
        assert d % ni == 0 and bi[0] % (2 * V7X_SUBLANES) == 0 and bi[1] % V7X_LANES == 0
        assert bo[0] % (2 * V7X_SUBLANES) == 0 and bo[1] % V7X_LANES == 0
        in_specs += [pl.BlockSpec((None, None) + bi, lambda i, j: (layer, half, i, j)),
                     pl.BlockSpec((None, None) + bo, lambda i, j: (layer, half, j, i))]
        out_specs += [pl.BlockSpec(bi, lambda i, j: (i, j)), pl.BlockSpec(bo, lambda i, j: (j, i))]
        out_shape += [jax.ShapeDtypeStruct((d, 2 * dff), BF16), jax.ShapeDtypeStruct((dff, d), BF16)]
        operands += [src_in, src_out]
    outs = pl.pallas_call(
        functools.partial(_ffn_kernel, apply_final=apply_final, cast_next=cast_next is not None,
                          row_chunk=row_chunk, col_chunk=col_chunk),
        grid=(ni, nj),
        in_specs=in_specs,
        out_specs=out_specs,
        out_shape=out_shape,
        scratch_shapes=[pltpu.VMEM((tm, d), BF16)],
        compiler_params=_params("arbitrary", "arbitrary"),
        name="ffn",
    )(*operands)
    return outs if cast_next is not None else outs[0]


def _gelu_tanh(x):
    return 0.5 * x * (1.0 + jnp.tanh(0.7978845608028654 * (x + 0.044715 * (x * x * x))))


_EPILOGUES = {
    None: lambda y: y,
    "gelu": _gelu_tanh,
    "sigmoid": jax.nn.sigmoid,
    "attn_scale": lambda y: y * (ATTN_SCALE * LOG2E),
}


def _proj2_kernel(x_ref, g_ref, w_ref, oa_ref, ob_ref, *, a_epilogue, b_epilogue):
    half = w_ref.shape[1] // 2
    xn = _rms(x_ref[...], g_ref[...]).astype(BF16)
    a = jnp.dot(xn, w_ref[:, 0:half], preferred_element_type=F32)
    oa_ref[...] = _EPILOGUES[a_epilogue](a).astype(oa_ref.dtype)
    b = jnp.dot(xn, w_ref[:, half:], preferred_element_type=F32)
    ob_ref[...] = _EPILOGUES[b_epilogue](b).astype(ob_ref.dtype)


def _proj2(x, g, w, *, tm, a_epilogue=None, b_epilogue=None, a_dtype=F32):
    n, d = x.shape
    half = w.shape[1] // 2
    assert n % tm == 0
    return pl.pallas_call(
        functools.partial(_proj2_kernel, a_epilogue=a_epilogue, b_epilogue=b_epilogue),
        grid=(n // tm,),
        in_specs=[
            pl.BlockSpec((tm, d), lambda i: (i, 0)),
            pl.BlockSpec((1, d), lambda i: (0, 0)),
            pl.BlockSpec(w.shape, lambda i: (0, 0), pipeline_mode=pl.Buffered(1)),
        ],
        out_specs=[pl.BlockSpec((tm, half), lambda i: (i, 0))] * 2,
        out_shape=[jax.ShapeDtypeStruct((n, half), a_dtype), jax.ShapeDtypeStruct((n, half), F32)],
        compiler_params=_params("arbitrary"),
        name="proj2",
    )(x, g.reshape(1, d), w)


def _log_sigmoid(z):
    return jnp.minimum(z, 0.0) - jnp.log1p(jnp.exp(-jnp.abs(z)))


def _kv_kernel(x_ref, g_ref, w_ref, wf_ref, bf_ref, k_ref, v_ref, f_ref):
    half = w_ref.shape[1] // 2
    xn = _rms(x_ref[...], g_ref[...]).astype(BF16)
    k_ref[...] = jnp.dot(xn, w_ref[:, 0:half], preferred_element_type=F32)
    v_ref[...] = jnp.dot(xn, w_ref[:, half:], preferred_element_type=F32)
    z = jnp.dot(xn, wf_ref[...], preferred_element_type=F32) + bf_ref[...]
    f_ref[...] = _log_sigmoid(z)


def _kv_proj(x, g, w_kv, wf_pad, bf_pad, *, tm):
    n, d = x.shape
    half = w_kv.shape[1] // 2
    assert n % tm == 0
    out = jax.ShapeDtypeStruct((n, half), F32)
    return pl.pallas_call(
        _kv_kernel,
        grid=(n // tm,),
        in_specs=[
            pl.BlockSpec((tm, d), lambda i: (i, 0)),
            pl.BlockSpec((1, d), lambda i: (0, 0)),
            pl.BlockSpec(w_kv.shape, lambda i: (0, 0), pipeline_mode=pl.Buffered(1)),
            pl.BlockSpec((d, V7X_LANES), lambda i: (0, 0)),
            pl.BlockSpec((1, V7X_LANES), lambda i: (0, 0)),
        ],
        out_specs=[
            pl.BlockSpec((tm, half), lambda i: (i, 0)),
            pl.BlockSpec((tm, half), lambda i: (i, 0)),
            pl.BlockSpec((tm, V7X_LANES), lambda i: (i, 0)),
        ],
        out_shape=[out, out, jax.ShapeDtypeStruct((n, V7X_LANES), F32)],
        compiler_params=_params("arbitrary"),
        name="kv_proj",
    )(x, g.reshape(1, d), w_kv, wf_pad, bf_pad)


def _out_proj_kernel(a_ref, w_ref, x_ref, o_ref):
    o_ref[...] = x_ref[...] + jnp.dot(a_ref[...], w_ref[...], preferred_element_type=F32)


def _out_proj(a, w, x, *, tm):
    n, k = a.shape
    d = w.shape[1]
    assert n % tm == 0
    return pl.pallas_call(
        _out_proj_kernel,
        grid=(n // tm,),
        in_specs=[
            pl.BlockSpec((tm, k), lambda i: (i, 0)),
            pl.BlockSpec((k, d), lambda i: (0, 0)),
            pl.BlockSpec((tm, d), lambda i: (i, 0)),
        ],
        out_specs=pl.BlockSpec((tm, d), lambda i: (i, 0)),
        out_shape=jax.ShapeDtypeStruct((n, d), F32),
        compiler_params=_params("arbitrary"),
        name="out_proj",
    )(a, w, x)


def _softplus(y):
    return jnp.maximum(y, 0.0) + jnp.log1p(jnp.exp(-jnp.abs(y)))


def _scan_rows(a_ref, b_ref, hin_ref, h_in):
    g = SCAN_GROUP
    groups = a_ref.shape[0] // g

    def phase(k):
        return pl.ds(k, groups, stride=g)

    a_acc, b_acc = a_ref[phase(0), :], b_ref[phase(0), :]
    for k in range(1, g):
        ak = a_ref[phase(k), :]
        b_acc = ak * b_acc + b_ref[phase(k), :]
        a_acc = ak * a_acc
        a_ref[phase(k), :] = a_acc
        b_ref[phase(k), :] = b_acc
    hc = h_in
    for gi in range(groups):
        hin_ref[gi:gi + 1, :] = hc
        hc = a_acc[gi:gi + 1, :] * hc + b_acc[gi:gi + 1, :]
    hin = hin_ref[...]
    for k in range(g):
        b_ref[phase(k), :] = a_ref[phase(k), :] * hin + b_ref[phase(k), :]
    return hc


def _rglru_kernel(gate_ref, u_ref, conv0_ref, h0_ref, cw_ref, cb_ref, gw_ref, gb_ref, lam_ref,
                  hg_ref, nconv_ref, nh_ref, ubuf, a_s, b_s, hin_s, hcar, *, tt):
    c = u_ref.shape[-1]
    pad = CONV_PAD_ROWS

    @pl.when(pl.program_id(1) == 0)
    def _():
        ubuf[0:pad, :] = conv0_ref[0]
        hcar[...] = h0_ref[0]

    ubuf[pad:pad + tt, :] = u_ref[0]

    for blk in range(c // RNN_BLOCK):
        cols = slice(blk * RNN_BLOCK, (blk + 1) * RNN_BLOCK)
        uc = cb_ref[:, cols]
        for k in range(CONV_W):
            uc = uc + ubuf[pad - (CONV_W - 1 - k):pad - (CONV_W - 1 - k) + tt, cols] * cw_ref[k:k + 1, cols]
        ub = uc.astype(BF16)
        r = jax.nn.sigmoid(jnp.dot(ub, gw_ref[0, blk], preferred_element_type=F32) + gb_ref[0:1, cols])
        i = jax.nn.sigmoid(jnp.dot(ub, gw_ref[1, blk], preferred_element_type=F32) + gb_ref[1:2, cols])
        log_a = (-LRU_C * r) * _softplus(-lam_ref[:, cols])
        a = jnp.exp(log_a)
        inp = jnp.sqrt(1.0 - a * a) * (i * uc)
        for j in range(RNN_BLOCK // V7X_LANES):
            slab = blk * (RNN_BLOCK // V7X_LANES) + j
            lanes = slice(slab * V7X_LANES, (slab + 1) * V7X_LANES)
            a_s[slab] = a[:, j * V7X_LANES:(j + 1) * V7X_LANES]
            b_s[slab] = inp[:, j * V7X_LANES:(j + 1) * V7X_LANES]
            hcar[:, lanes] = _scan_rows(a_s.at[slab], b_s.at[slab], hin_s.at[slab], hcar[:, lanes])
            hg_ref[0, :, lanes] = (b_s[slab] * gate_ref[0, :, lanes]).astype(BF16)

    tail = ubuf[pad + tt - (CONV_W - 1):pad + tt, :]
    nconv_ref[0] = tail
    ubuf[pad - (CONV_W - 1):pad, :] = tail
    nh_ref[0] = hcar[...]


def _rglru(gate, u, conv0, h0, cw, cb, gw, gb, lam, *, tt):
    b, t, c = u.shape
    assert t % tt == 0 and tt % V7X_SUBLANES == 0 and tt >= CONV_W - 1
    nblk = c // RNN_BLOCK
    conv0_pad = jnp.pad(conv0, ((0, 0), (CONV_PAD_ROWS - (CONV_W - 1), 0), (0, 0)))
    return pl.pallas_call(
        functools.partial(_rglru_kernel, tt=tt),
        grid=(b, t // tt),
        in_specs=[
            pl.BlockSpec((1, tt, c), lambda bi, ti: (bi, ti, 0)),
            pl.BlockSpec((1, tt, c), lambda bi, ti: (bi, ti, 0)),
            pl.BlockSpec((1, CONV_PAD_ROWS, c), lambda bi, ti: (bi, 0, 0)),
            pl.BlockSpec((1, 1, c), lambda bi, ti: (bi, 0, 0)),
            pl.BlockSpec((CONV_W, c), lambda bi, ti: (0, 0)),
            pl.BlockSpec((1, c), lambda bi, ti: (0, 0)),
            pl.BlockSpec((2, nblk, RNN_BLOCK, RNN_BLOCK), lambda bi, ti: (0, 0, 0, 0)),
            pl.BlockSpec((2, c), lambda bi, ti: (0, 0)),
            pl.BlockSpec((1, c), lambda bi, ti: (0, 0)),
        ],
        out_specs=[
            pl.BlockSpec((1, tt, c), lambda bi, ti: (bi, ti, 0)),
            pl.BlockSpec((1, CONV_W - 1, c), lambda bi, ti: (bi, 0, 0)),
            pl.BlockSpec((1, 1, c), lambda bi, ti: (bi, 0, 0)),
        ],
        out_shape=[
            jax.ShapeDtypeStruct((b, t, c), BF16),
            jax.ShapeDtypeStruct((b, CONV_W - 1, c), F32),
            jax.ShapeDtypeStruct((b, 1, c), F32),
        ],
        scratch_shapes=[
            pltpu.VMEM((CONV_PAD_ROWS + tt, c), F32),
            pltpu.VMEM((c // V7X_LANES, tt, V7X_LANES), F32),
            pltpu.VMEM((c // V7X_LANES, tt, V7X_LANES), F32),
            pltpu.VMEM((c // V7X_LANES, tt // SCAN_GROUP, V7X_LANES), F32),
            pltpu.VMEM((1, c), F32),
        ],
        compiler_params=_params("arbitrary", "arbitrary"),
        name="rglru",
    )(gate, u, conv0_pad, h0.reshape(b, 1, c), cw, cb.reshape(1, c), gw, gb, lam.reshape(1, c))


def _cumsum_kernel(x_ref, o_ref, *, rows):
    s = x_ref.shape[1]
    r = lax.broadcasted_iota(jnp.int32, (rows, rows), 0)
    q = lax.broadcasted_iota(jnp.int32, (rows, rows), 1)
    tril = (q <= r).astype(F32)

    def block(bi, carry):
        r0 = pl.multiple_of(bi * rows, rows)
        cs = jnp.dot(tril, x_ref[0, pl.ds(r0, rows), :], preferred_element_type=F32,
                     precision=lax.Precision.HIGHEST) + carry
        o_ref[0, pl.ds(r0, rows), :] = cs
        return cs[rows - 1:rows, :]

    lax.fori_loop(0, s // rows, block, jnp.zeros((1, x_ref.shape[2]), F32))


def _cumsum(x, *, rows):
    b, s, w = x.shape
    assert s % rows == 0
    return pl.pallas_call(
        functools.partial(_cumsum_kernel, rows=rows),
        grid=(b,),
        in_specs=[pl.BlockSpec((1, s, w), lambda bi: (bi, 0, 0))],
        out_specs=pl.BlockSpec((1, s, w), lambda bi: (bi, 0, 0)),
        out_shape=jax.ShapeDtypeStruct((b, s, w), F32),
        compiler_params=_params("arbitrary"),
        name="cumsum",
    )(x)


def _split3(c):
    hi = c.astype(BF16)
    r1 = c - hi.astype(F32)
    mid = r1.astype(BF16)
    lo = (r1 - mid.astype(F32)).astype(BF16)
    return hi, mid, lo


def _aug_pieces(c, head):
    w = c.shape[1]
    src = lax.broadcasted_iota(jnp.int32, (w, V7X_LANES), 0)
    dst = lax.broadcasted_iota(jnp.int32, (w, V7X_LANES), 1)
    out = None
    for piece, q_lane, k_lane in zip(_split3(c * LOG2E), AUG_Q_LANES, AUG_K_LANES):
        sign = jnp.where(dst == q_lane, 1.0, jnp.where(dst == k_lane, -1.0, 0.0))
        route = jnp.where(src == head, sign, 0.0).astype(BF16)
        term = jnp.dot(piece, route, preferred_element_type=F32)
        out = term if out is None else out + term
    return out


def _aug_side(pieces, one_lanes):
    lane_id = lax.broadcasted_iota(jnp.int32, pieces.shape, 1)
    ones = (lane_id >= one_lanes[0]) & (lane_id <= one_lanes[-1])
    return jnp.where(ones, 1.0, pieces).astype(BF16)


def _attn_prompt_kernel(q_ref, k_ref, v_ref, g_ref, c_ref, o_ref,
                        qa_ref, ka_ref, qx_ref, vt_ref, st_ref, mt_ref, m_ref, acc_ref, *, tile):
    head = pl.program_id(1)
    qi = pl.program_id(2)
    n_kt = vt_ref.shape[0]
    dims = (((1,), (1,)), ((), ()))

    @pl.when(qi == 0)
    def _():
        def build(bi, carry):
            rows = pl.ds(pl.multiple_of(bi * tile, tile), tile)
            ka_ref[rows, 0:HEAD_DIM] = k_ref[0, rows, :].astype(BF16)
            pieces = _aug_pieces(c_ref[0, rows, :], head)
            ka_ref[rows, HEAD_DIM:] = _aug_side(pieces, AUG_Q_LANES)
            qx_ref[rows, :] = _aug_side(pieces, AUG_K_LANES)
            vt_ref[bi, 0:HEAD_DIM, :] = v_ref[0, rows, :].T.astype(BF16)
            vt_ref[bi, HEAD_DIM:, :] = jnp.ones((ONES_ROWS, tile), BF16)
            return carry
        lax.fori_loop(0, n_kt, build, 0)

    q0 = pl.multiple_of(qi * (2 * tile), 2 * tile)
    qa_ref[:, 0:HEAD_DIM] = q_ref[0]
    qa_ref[:, HEAD_DIM:] = qx_ref[pl.ds(q0, 2 * tile), :]
    m_ref[...] = jnp.full(m_ref.shape, MASK_VALUE, F32)
    acc_ref[...] = jnp.zeros(acc_ref.shape, F32)

    def scores(half, kj, slot):
        k0 = pl.multiple_of(kj * tile, tile)
        st = lax.dot_general(ka_ref[pl.ds(k0, tile), :], qa_ref[half * tile:(half + 1) * tile, :], dims,
                             preferred_element_type=F32)
        st_ref[slot] = st
        mt_ref[slot] = jnp.max(st, axis=0, keepdims=True)

    def update(half, kj, slot, masked):
        st = st_ref[slot]
        if masked:
            key = lax.broadcasted_iota(jnp.int32, (tile, tile), 0)
            qry = lax.broadcasted_iota(jnp.int32, (tile, tile), 1)
            st = jnp.where(key <= qry, st, MASK_VALUE)
            m_tile = jnp.max(st, axis=0, keepdims=True)
        else:
            m_tile = mt_ref[slot]
        m_prev = m_ref[half]
        m_new = jnp.maximum(m_prev, m_tile)
        alpha = jnp.exp2(m_prev - m_new)
        p = jnp.exp2(st - m_new).astype(BF16)
        acc_ref[half] = alpha * acc_ref[half] + jnp.dot(vt_ref[kj], p, preferred_element_type=F32)
        m_ref[half] = m_new

    scores(0, 0, 0)

    def full_tiles(kj, carry):
        scores(1, kj, 1)
        update(0, kj, 0, False)
        scores(0, kj + 1, 0)
        update(1, kj, 1, False)
        return carry

    lax.fori_loop(0, 2 * qi, full_tiles, 0)
    scores(1, 2 * qi, 1)
    update(0, 2 * qi, 0, True)
    scores(1, 2 * qi + 1, 0)
    update(1, 2 * qi, 1, False)
    update(1, 2 * qi + 1, 0, True)
    for half in range(2):
        rows = slice(half * tile, (half + 1) * tile)
        o = (acc_ref[half, 0:HEAD_DIM, :] / acc_ref[half, HEAD_DIM:HEAD_DIM + 1, :]).T
        o_ref[0, rows, :] = (o * g_ref[0, rows, :]).astype(BF16)


def _attn_prompt(q, k, v, g, c, *, tile):
    b, s, d = q.shape
    nh = d // HEAD_DIM
    assert s % (2 * tile) == 0 and tile % V7X_LANES == 0
    qspec = pl.BlockSpec((1, 2 * tile, HEAD_DIM), lambda bi, hi, qi: (bi, qi, hi))
    kspec = pl.BlockSpec((1, s, HEAD_DIM), lambda bi, hi, qi: (bi, 0, hi))
    return pl.pallas_call(
        functools.partial(_attn_prompt_kernel, tile=tile),
        grid=(b, nh, s // (2 * tile)),
        in_specs=[qspec, kspec, kspec, qspec,
                  pl.BlockSpec((1, s, V7X_LANES), lambda bi, hi, qi: (bi, 0, 0))],
        out_specs=qspec,
        out_shape=jax.ShapeDtypeStruct((b, s, d), BF16),
        scratch_shapes=[
            pltpu.VMEM((2 * tile, 2 * HEAD_DIM), BF16),
            pltpu.VMEM((s, 2 * HEAD_DIM), BF16),
            pltpu.VMEM((s, HEAD_DIM), BF16),
            pltpu.VMEM((s // tile, HEAD_DIM + ONES_ROWS, tile), BF16),
            pltpu.VMEM((2, tile, tile), F32),
            pltpu.VMEM((2, 1, tile), F32),
            pltpu.VMEM((2, 1, tile), F32),
            pltpu.VMEM((2, HEAD_DIM + ONES_ROWS, tile), F32),
        ],
        compiler_params=_params("arbitrary", "arbitrary", "arbitrary"),
        name="attn_prompt",
    )(q, k, v, g, c)


def _attn_sample_kernel(q_ref, kc_ref, vc_ref, kn_ref, vn_ref, g_ref, c_ref, o_ref):
    t = q_ref.shape[1]
    nh = q_ref.shape[2] // HEAD_DIM
    past = kc_ref.shape[1] // nh
    dims = (((1,), (1,)), ((), ()))
    row = lax.broadcasted_iota(jnp.int32, (t, t), 0)
    col = lax.broadcasted_iota(jnp.int32, (t, t), 1)
    c_all = c_ref[0]
    c_past, c_new = c_all[0:past], c_all[past:past + t]
    def scores(head):
        cols = slice(head * HEAD_DIM, (head + 1) * HEAD_DIM)
        new_pieces = _aug_pieces(c_new, head)
        qa = jnp.concatenate([q_ref[0, :, cols], _aug_side(new_pieces, AUG_K_LANES)], axis=1)
        kca = jnp.concatenate([kc_ref[0, pl.ds(head, past, stride=nh), :].astype(BF16),
                               _aug_side(_aug_pieces(c_past, head), AUG_Q_LANES)], axis=1)
        kna = jnp.concatenate([kn_ref[0, :, cols].astype(BF16), _aug_side(new_pieces, AUG_Q_LANES)], axis=1)
        s_past = lax.dot_general(qa, kca, dims, preferred_element_type=F32)
        s_new = jnp.where(col <= row, lax.dot_general(qa, kna, dims, preferred_element_type=F32), MASK_VALUE)
        return s_past, s_new

    def finish(head, s_past, s_new):
        cols = slice(head * HEAD_DIM, (head + 1) * HEAD_DIM)
        m = jnp.maximum(jnp.max(s_past, axis=1, keepdims=True), jnp.max(s_new, axis=1, keepdims=True))
        p_past = jnp.exp2(s_past - m)
        p_new = jnp.exp2(s_new - m)
        l = jnp.sum(p_past, axis=1, keepdims=True) + jnp.sum(p_new, axis=1, keepdims=True)
        v_past = vc_ref[0, pl.ds(head, past, stride=nh), :].astype(BF16)
        o = (jnp.dot(p_past.astype(BF16), v_past, preferred_element_type=F32)
             + jnp.dot(p_new.astype(BF16), vn_ref[0, :, cols].astype(BF16), preferred_element_type=F32)) / l
        o_ref[0, :, cols] = (o * g_ref[0, :, cols]).astype(BF16)

    pending = scores(0)
    for head in range(nh):
        current, pending = pending, (scores(head + 1) if head + 1 < nh else None)
        finish(head, *current)


def _attn_sample(q, kc, vc, kn, vn, g, c):
    b, t, d = q.shape
    _, past, nh, hd = kc.shape
    assert nh % V7X_SUBLANES == 0
    kc, vc = kc.reshape(b, past * nh, hd), vc.reshape(b, past * nh, hd)
    new = pl.BlockSpec((1, t, d), lambda bi: (bi, 0, 0))
    old = pl.BlockSpec((1, past * nh, hd), lambda bi: (bi, 0, 0))
    return pl.pallas_call(
        _attn_sample_kernel,
        grid=(b,),
        in_specs=[new, old, old, new, new, new,
                  pl.BlockSpec((1, past + t, V7X_LANES), lambda bi: (bi, 0, 0))],
        out_specs=new,
        out_shape=jax.ShapeDtypeStruct((b, t, d), BF16),
        compiler_params=_params("arbitrary"),
        name="attn_sample",
    )(q, kc, vc, kn, vn, g, c)


FFN_TM, FFN_TF = 1024, 512
NORM_ROW_CHUNK, FFN_COL_CHUNK = 256, 512
PROJ_TM = 256
OUT_TM = 512
RGLRU_TT_PROMPT = 256
CUMSUM_ROWS_PROMPT, CUMSUM_ROWS_SAMPLE = 256, 272
ATTN_TILE = 512


def kernel(x_prompt, x_sample, state_conv, state_h, cache_k, cache_v, cache_logf, ffn_norm, ffn_w_in,
           ffn_w_out, a_norm, a_w_in, a_conv_w, a_conv_b, a_gate_w, a_gate_b, a_lambda, a_w_out, kv_norm,
           w_kv, w_f, b_f, b_norm, b_w_qg, b_w_o, final_norm):
    bp, sp, d = x_prompt.shape
    bs, ts, _ = x_sample.shape
    depth = ffn_w_in.shape[0]
    n_a = a_w_in.shape[0]
    n_heads = w_f.shape[1]

    ffn_order = [(l, half) for l in range(depth) for half in range(2)]
    ffn_w = {ffn_order[0]: (ffn_w_in[0, 0].astype(BF16), ffn_w_out[0, 0].astype(BF16))}
    a_w_in_b, a_gate_w_b, a_w_out_b = a_w_in.astype(BF16), a_gate_w.astype(BF16), a_w_out.astype(BF16)
    w_kv_b, b_w_qg_b, b_w_o_b = w_kv.astype(BF16), b_w_qg.astype(BF16), b_w_o.astype(BF16)
    wf_pad = jnp.pad(w_f, ((0, 0), (0, V7X_LANES - n_heads))).astype(BF16)
    bf_pad = jnp.pad(b_f, (0, V7X_LANES - n_heads)).reshape(1, V7X_LANES)

    streams = {"p": x_prompt.reshape(bp * sp, d), "s": x_sample.reshape(bs * ts, d)}
    batch = {"p": (bp, sp), "s": (bs, ts)}
    conv_out = {"p": [], "s": []}
    h_out = {"p": [], "s": []}
    kv = {}

    def ffn(x, l, half, name, apply_final=False):
        w_in_b, w_out_b = ffn_w[(l, half)]
        pos = ffn_order.index((l, half))
        cast_next = None
        if name == "p" and pos + 1 < len(ffn_order):
            cast_next = (ffn_w_in, ffn_w_out) + ffn_order[pos + 1]
        out = _ffn(x, ffn_norm[l, half], w_in_b, w_out_b, final_norm, apply_final=apply_final, tm=FFN_TM,
                   tf=FFN_TF, cast_next=cast_next)
        if cast_next is None:
            return out
        ffn_w[ffn_order[pos + 1]] = (out[1], out[2])
        return out[0]

    for l in range(depth):
        for name in ("p", "s"):
            x = ffn(streams[name], l, 0, name)
            b, t = batch[name]
            if l < n_a:
                gate, u = _proj2(x, a_norm[l], a_w_in_b[l], tm=PROJ_TM, a_epilogue="gelu")
                c_rnn = u.shape[1]
                if name == "p":
                    conv0 = jnp.zeros((b, CONV_W - 1, c_rnn), F32)
                    h0 = jnp.zeros((b, c_rnn), F32)
                    tt = RGLRU_TT_PROMPT
                else:
                    conv0, h0, tt = state_conv[l], state_h[l], t
                hg, nconv, nh = _rglru(gate.reshape(b, t, c_rnn), u.reshape(b, t, c_rnn), conv0, h0,
                                       a_conv_w[l], a_conv_b[l], a_gate_w_b[l], a_gate_b[l], a_lambda[l], tt=tt)
                conv_out[name].append(nconv)
                h_out[name].append(nh.reshape(b, c_rnn))
                x = _out_proj(hg.reshape(b * t, c_rnn), a_w_out_b[l], x, tm=OUT_TM)
            else:
                jb = l - n_a
                q, g = _proj2(x, b_norm[jb], b_w_qg_b[jb], tm=PROJ_TM, a_epilogue="attn_scale",
                              b_epilogue="sigmoid", a_dtype=BF16)
                k, v, f_pad = kv[name]
                if name == "p":
                    c = _cumsum(f_pad.reshape(b, t, V7X_LANES), rows=CUMSUM_ROWS_PROMPT)
                    og = _attn_prompt(q.reshape(b, t, d), k.reshape(b, t, d), v.reshape(b, t, d),
                                      g.reshape(b, t, d), c, tile=ATTN_TILE)
                else:
                    cache_f_pad = jnp.pad(cache_logf, ((0, 0), (0, 0), (0, V7X_LANES - n_heads)))
                    f_all = jnp.concatenate([cache_f_pad, f_pad.reshape(b, t, V7X_LANES)], axis=1)
                    c = _cumsum(f_all, rows=CUMSUM_ROWS_SAMPLE)
                    og = _attn_sample(q.reshape(b, t, d), cache_k, cache_v, k.reshape(b, t, d),
                                      v.reshape(b, t, d), g.reshape(b, t, d), c)
                x = _out_proj(og.reshape(b * t, d), b_w_o_b[jb], x, tm=OUT_TM)
            x = ffn(x, l, 1, name, apply_final=(l == depth - 1))
            if l == n_a - 1:
                kv[name] = _kv_proj(x, kv_norm, w_kv_b, wf_pad, bf_pad, tm=PROJ_TM)
            streams[name] = x

    def finish(name):
        b, t = batch[name]
        k, v, f_pad = kv[name]
        return (streams[name].reshape(b, t, d), jnp.stack(conv_out[name]), jnp.stack(h_out[name]),
                k.reshape(b, t, n_heads, HEAD_DIM), v.reshape(b, t, n_heads, HEAD_DIM),
                f_pad[:, :n_heads].reshape(b, t, n_heads))

    yp, convp, hp, kp, vp, fp = finish("p")
    ys, convs, hs, ks, vs, fs = finish("s")
    return (yp, ys, convp, hp, kp, vp, fp, convs, hs, ks, vs, fs)
```

```python
import functools

import jax
import jax.numpy as jnp
from jax import lax
from jax.experimental import pallas as pl
from jax.experimental.pallas import tpu as pltpu

F32 = jnp.float32
BF16 = jnp.bfloat16

RMS_EPS = 1e-6
LRU_C = 8.0
HEAD_DIM = 128
ATTN_SCALE = HEAD_DIM ** -0.5
LOG2E = 1.4426950408889634
MASK_VALUE = -1e30

V7X_LANES = 128
V7X_SUBLANES = 8
V7X_VMEM_BYTES = 64 * 1024 * 1024
VMEM_LIMIT_BYTES = V7X_VMEM_BYTES - 5 * 1024 * 1024

RNN_BLOCK = 256
SCAN_GROUP = 8
CONV_W = 4
CONV_PAD_ROWS = V7X_SUBLANES

AUG_Q_LANES = (0, 1, 2)
AUG_K_LANES = (3, 4, 5)
ONES_ROWS = 16


def _params(*semantics):
    return pltpu.CompilerParams(dimension_semantics=semantics, vmem_limit_bytes=VMEM_LIMIT_BYTES)


def _rms(x, g):
    y = x * lax.rsqrt(jnp.mean(x * x, axis=-1, keepdims=True) + RMS_EPS)
    return y * g


def _ffn_kernel(x_ref, g_ref, wg_ref, wu_ref, wo_ref, fin_ref, *rest, apply_final, cast_next, row_chunk,
                col_chunk):
    if cast_next:
        src_in_ref, src_out_ref, o_ref, dst_in_ref, dst_out_ref, xn_ref = rest
        dst_in_ref[...] = src_in_ref[...].astype(BF16)
        dst_out_ref[...] = src_out_ref[...].astype(BF16)
    else:
        o_ref, xn_ref = rest
    j = pl.program_id(1)
    tm, d = x_ref.shape

    def by_rows(fn):
        def body(ri, carry):
            fn(pl.ds(pl.multiple_of(ri * row_chunk, row_chunk), row_chunk))
            return carry
        lax.fori_loop(0, tm // row_chunk, body, 0)

    @pl.when(j == 0)
    def _():
        def prep(rows):
            x = x_ref[rows, :]
            xn_ref[rows, :] = _rms(x, g_ref[...]).astype(BF16)
            o_ref[rows, :] = x
        by_rows(prep)

    xn = xn_ref[...]
    gate = jnp.dot(xn, wg_ref[...], preferred_element_type=F32)
    up = jnp.dot(xn, wu_ref[...], preferred_element_type=F32)
    h = (0.5 * (gate * jax.nn.sigmoid(gate)) * up).astype(BF16)
    for c in range(d // col_chunk):
        cols = slice(c * col_chunk, (c + 1) * col_chunk)
        o_ref[:, cols] += jnp.dot(h, wo_ref[:, cols], preferred_element_type=F32)

    if apply_final:
        @pl.when(j == pl.num_programs(1) - 1)
        def _():
            def norm(rows):
                o_ref[rows, :] = _rms(o_ref[rows, :], fin_ref[...])
            by_rows(norm)


def _ffn(x, g, w_in, w_out, fin, *, apply_final, tm, tf, cast_next=None):
    n, d = x.shape
    dff = w_out.shape[0]
    ni, nj = n // tm, dff // tf
    assert n % tm == 0 and dff % tf == 0 and w_in.shape == (d, 2 * dff)
    row_chunk, col_chunk = min(NORM_ROW_CHUNK, tm), min(FFN_COL_CHUNK, d)
    assert tm % row_chunk == 0 and d % col_chunk == 0
    in_specs = [
        pl.BlockSpec((tm, d), lambda i, j: (i, 0)),
        pl.BlockSpec((1, d), lambda i, j: (0, 0)),
        pl.BlockSpec((d, tf), lambda i, j: (0, j)),
        pl.BlockSpec((d, tf), lambda i, j: (0, j + nj)),
        pl.BlockSpec((tf, d), lambda i, j: (j, 0)),
        pl.BlockSpec((1, d), lambda i, j: (0, 0)),
    ]
    out_specs = [pl.BlockSpec((tm, d), lambda i, j: (i, 0))]
    out_shape = [jax.ShapeDtypeStruct((n, d), F32)]
    operands = [x, g.reshape(1, d), w_in, w_in, w_out, fin.reshape(1, d)]
    if cast_next is not None:
        src_in, src_out, layer, half = cast_next
        bi, bo = (d // ni, 2 * dff // nj), (dff // nj, d // ni)
        assert d % ni == 0 and bi[0] % (2 * V7X_SUBLANES) == 0 and bi[1] % V7X_LANES == 0
        assert bo[0] % (2 * V7X_SUBLANES) == 0 and bo[1] % V7X_LANES == 0
        in_specs += [pl.BlockSpec((None, None) + bi, lambda i, j: (layer, half, i, j)),
                     pl.BlockSpec((None, None) + bo, lambda i, j: (layer, half, j, i))]
        out_specs += [pl.BlockSpec(bi, lambda i, j: (i, j)), pl.BlockSpec(bo, lambda i, j: (j, i))]
        out_shape += [jax.ShapeDtypeStruct((d, 2 * dff), BF16), jax.ShapeDtypeStruct((dff, d), BF16)]
        operands += [src_in, src_out]
    outs = pl.pallas_call(
        functools.partial(_ffn_kernel, apply_final=apply_final, cast_next=cast_next is not None,
                          row_chunk=row_chunk, col_chunk=col_chunk),
        grid=(ni, nj),
        in_specs=in_specs,
        out_specs=out_specs,
        out_shape=out_shape,
        scratch_shapes=[pltpu.VMEM((tm, d), BF16)],
        compiler_params=_params("arbitrary", "arbitrary"),
        name="ffn",
    )(*operands)
    return outs if cast_next is not None else outs[0]


def _gelu_tanh(x):
    return 0.5 * x * (1.0 + jnp.tanh(0.7978845608028654 * (x + 0.044715 * (x * x * x))))


_EPILOGUES = {
    None: lambda y: y,
    "gelu": _gelu_tanh,
    "sigmoid": jax.nn.sigmoid,
    "attn_scale": lambda y: y * (ATTN_SCALE * LOG2E),
}


def _proj2_kernel(x_ref, g_ref, w_ref, oa_ref, ob_ref, *, a_epilogue, b_epilogue):
    half = w_ref.shape[1] // 2
    xn = _rms(x_ref[...], g_ref[...]).astype(BF16)
    a = jnp.dot(xn, w_ref[:, 0:half], preferred_element_type=F32)
    oa_ref[...] = _EPILOGUES[a_epilogue](a).astype(oa_ref.dtype)
    b = jnp.dot(xn, w_ref[:, half:], preferred_element_type=F32)
    ob_ref[...] = _EPILOGUES[b_epilogue](b).astype(ob_ref.dtype)


def _proj2(x, g, w, *, tm, a_epilogue=None, b_epilogue=None, a_dtype=F32):
    n, d = x.shape
    half = w.shape[1] // 2
    assert n % tm == 0
    return pl.pallas_call(
        functools.partial(_proj2_kernel, a_epilogue=a_epilogue, b_epilogue=b_epilogue),
        grid=(n // tm,),
        in_specs=[
            pl.BlockSpec((tm, d), lambda i: (i, 0)),
            pl.BlockSpec((1, d), lambda i: (0, 0)),
            pl.BlockSpec(w.shape, lambda i: (0, 0), pipeline_mode=pl.Buffered(1)),
        ],
        out_specs=[pl.BlockSpec((tm, half), lambda i: (i, 0))] * 2,
        out_shape=[jax.ShapeDtypeStruct((n, half), a_dtype), jax.ShapeDtypeStruct((n, half), F32)],
        compiler_params=_params("arbitrary"),
        name="proj2",
    )(x, g.reshape(1, d), w)


def _log_sigmoid(z):
    return jnp.minimum(z, 0.0) - jnp.log1p(jnp.exp(-jnp.abs(z)))


def _kv_kernel(x_ref, g_ref, w_ref, wf_ref, bf_ref, k_ref, v_ref, f_ref, *head_major_refs):
    half = w_ref.shape[1] // 2
    xn = _rms(x_ref[...], g_ref[...]).astype(BF16)
    k = jnp.dot(xn, w_ref[:, 0:half], preferred_element_type=F32)
    v = jnp.dot(xn, w_ref[:, half:], preferred_element_type=F32)
    k_ref[...] = k
    v_ref[...] = v
    z = jnp.dot(xn, wf_ref[...], preferred_element_type=F32) + bf_ref[...]
    f_ref[...] = _log_sigmoid(z)
    if head_major_refs:
        kh_ref, vh_ref = head_major_refs
        for head in range(half // HEAD_DIM):
            cols = slice(head * HEAD_DIM, (head + 1) * HEAD_DIM)
            kh_ref[head] = k[:, cols].astype(BF16)
            vh_ref[head] = v[:, cols].astype(BF16)


def _kv_proj(x, g, w_kv, wf_pad, bf_pad, *, tm, head_major):
    n, d = x.shape
    half = w_kv.shape[1] // 2
    nh = half // HEAD_DIM
    assert n % tm == 0
    out = jax.ShapeDtypeStruct((n, half), F32)
    out_specs = [
        pl.BlockSpec((tm, half), lambda i: (i, 0)),
        pl.BlockSpec((tm, half), lambda i: (i, 0)),
        pl.BlockSpec((tm, V7X_LANES), lambda i: (i, 0)),
    ]
    out_shape = [out, out, jax.ShapeDtypeStruct((n, V7X_LANES), F32)]
    if head_major:
        out_specs += [pl.BlockSpec((nh, tm, HEAD_DIM), lambda i: (0, i, 0))] * 2
        out_shape += [jax.ShapeDtypeStruct((nh, n, HEAD_DIM), BF16)] * 2
    return pl.pallas_call(
        _kv_kernel,
        grid=(n // tm,),
        in_specs=[
            pl.BlockSpec((tm, d), lambda i: (i, 0)),
            pl.BlockSpec((1, d), lambda i: (0, 0)),
            pl.BlockSpec(w_kv.shape, lambda i: (0, 0), pipeline_mode=pl.Buffered(1)),
            pl.BlockSpec((d, V7X_LANES), lambda i: (0, 0)),
            pl.BlockSpec((1, V7X_LANES), lambda i: (0, 0)),
        ],
        out_specs=out_specs,
        out_shape=out_shape,
        compiler_params=_params("arbitrary"),
        name="kv_proj",
    )(x, g.reshape(1, d), w_kv, wf_pad, bf_pad)


def _out_proj_kernel(a_ref, w_ref, x_ref, o_ref):
    o_ref[...] = x_ref[...] + jnp.dot(a_ref[...], w_ref[...], preferred_element_type=F32)


def _out_proj(a, w, x, *, tm):
    n, k = a.shape
    d = w.shape[1]
    assert n % tm == 0
    return pl.pallas_call(
        _out_proj_kernel,
        grid=(n // tm,),
        in_specs=[
            pl.BlockSpec((tm, k), lambda i: (i, 0)),
            pl.BlockSpec((k, d), lambda i: (0, 0)),
            pl.BlockSpec((tm, d), lambda i: (i, 0)),
        ],
        out_specs=pl.BlockSpec((tm, d), lambda i: (i, 0)),
        out_shape=jax.ShapeDtypeStruct((n, d), F32),
        compiler_params=_params("arbitrary"),
        name="out_proj",
    )(a, w, x)


def _softplus(y):
    return jnp.maximum(y, 0.0) + jnp.log1p(jnp.exp(-jnp.abs(y)))


def _scan_rows(a_ref, b_ref, hin_ref, h_in):
    g = SCAN_GROUP
    groups = a_ref.shape[0] // g

    def phase(k):
        return pl.ds(k, groups, stride=g)

    a_acc, b_acc = a_ref[phase(0), :], b_ref[phase(0), :]
    for k in range(1, g):
        ak = a_ref[phase(k), :]
        b_acc = ak * b_acc + b_ref[phase(k), :]
        a_acc = ak * a_acc
        a_ref[phase(k), :] = a_acc
        b_ref[phase(k), :] = b_acc
    hc = h_in
    for gi in range(groups):
        hin_ref[gi:gi + 1, :] = hc
        hc = a_acc[gi:gi + 1, :] * hc + b_acc[gi:gi + 1, :]
    hin = hin_ref[...]
    for k in range(g):
        b_ref[phase(k), :] = a_ref[phase(k), :] * hin + b_ref[phase(k), :]
    return hc


def _rglru_kernel(gate_ref, u_ref, conv0_ref, h0_ref, cw_ref, cb_ref, gw_ref, gb_ref, lam_ref,
                  hg_ref, nconv_ref, nh_ref, ubuf, a_s, b_s, hin_s, hcar, *, tt):
    c = u_ref.shape[-1]
    pad = CONV_PAD_ROWS

    @pl.when(pl.program_id(1) == 0)
    def _():
        ubuf[0:pad, :] = conv0_ref[0]
        hcar[...] = h0_ref[0]

    ubuf[pad:pad + tt, :] = u_ref[0]

    for blk in range(c // RNN_BLOCK):
        cols = slice(blk * RNN_BLOCK, (blk + 1) * RNN_BLOCK)
        uc = cb_ref[:, cols]
        for k in range(CONV_W):
            uc = uc + ubuf[pad - (CONV_W - 1 - k):pad - (CONV_W - 1 - k) + tt, cols] * cw_ref[k:k + 1, cols]
        ub = uc.astype(BF16)
        r = jax.nn.sigmoid(jnp.dot(ub, gw_ref[0, blk], preferred_element_type=F32) + gb_ref[0:1, cols])
        i = jax.nn.sigmoid(jnp.dot(ub, gw_ref[1, blk], preferred_element_type=F32) + gb_ref[1:2, cols])
        log_a = (-LRU_C * r) * _softplus(-lam_ref[:, cols])
        a = jnp.exp(log_a)
        inp = jnp.sqrt(1.0 - a * a) * (i * uc)
        for j in range(RNN_BLOCK // V7X_LANES):
            slab = blk * (RNN_BLOCK // V7X_LANES) + j
            lanes = slice(slab * V7X_LANES, (slab + 1) * V7X_LANES)
            a_s[slab] = a[:, j * V7X_LANES:(j + 1) * V7X_LANES]
            b_s[slab] = inp[:, j * V7X_LANES:(j + 1) * V7X_LANES]
            hcar[:, lanes] = _scan_rows(a_s.at[slab], b_s.at[slab], hin_s.at[slab], hcar[:, lanes])
            hg_ref[0, :, lanes] = (b_s[slab] * gate_ref[0, :, lanes]).astype(BF16)

    tail = ubuf[pad + tt - (CONV_W - 1):pad + tt, :]
    nconv_ref[0] = tail
    ubuf[pad - (CONV_W - 1):pad, :] = tail
    nh_ref[0] = hcar[...]


def _rglru(gate, u, conv0, h0, cw, cb, gw, gb, lam, *, tt):
    b, t, c = u.shape
    assert t % tt == 0 and tt % V7X_SUBLANES == 0 and tt >= CONV_W - 1
    nblk = c // RNN_BLOCK
    conv0_pad = jnp.pad(conv0, ((0, 0), (CONV_PAD_ROWS - (CONV_W - 1), 0), (0, 0)))
    return pl.pallas_call(
        functools.partial(_rglru_kernel, tt=tt),
        grid=(b, t // tt),
        in_specs=[
            pl.BlockSpec((1, tt, c), lambda bi, ti: (bi, ti, 0)),
            pl.BlockSpec((1, tt, c), lambda bi, ti: (bi, ti, 0)),
            pl.BlockSpec((1, CONV_PAD_ROWS, c), lambda bi, ti: (bi, 0, 0)),
            pl.BlockSpec((1, 1, c), lambda bi, ti: (bi, 0, 0)),
            pl.BlockSpec((CONV_W, c), lambda bi, ti: (0, 0)),
            pl.BlockSpec((1, c), lambda bi, ti: (0, 0)),
            pl.BlockSpec((2, nblk, RNN_BLOCK, RNN_BLOCK), lambda bi, ti: (0, 0, 0, 0)),
            pl.BlockSpec((2, c), lambda bi, ti: (0, 0)),
            pl.BlockSpec((1, c), lambda bi, ti: (0, 0)),
        ],
        out_specs=[
            pl.BlockSpec((1, tt, c), lambda bi, ti: (bi, ti, 0)),
            pl.BlockSpec((1, CONV_W - 1, c), lambda bi, ti: (bi, 0, 0)),
            pl.BlockSpec((1, 1, c), lambda bi, ti: (bi, 0, 0)),
        ],
        out_shape=[
            jax.ShapeDtypeStruct((b, t, c), BF16),
            jax.ShapeDtypeStruct((b, CONV_W - 1, c), F32),
            jax.ShapeDtypeStruct((b, 1, c), F32),
        ],
        scratch_shapes=[
            pltpu.VMEM((CONV_PAD_ROWS + tt, c), F32),
            pltpu.VMEM((c // V7X_LANES, tt, V7X_LANES), F32),
            pltpu.VMEM((c // V7X_LANES, tt, V7X_LANES), F32),
            pltpu.VMEM((c // V7X_LANES, tt // SCAN_GROUP, V7X_LANES), F32),
            pltpu.VMEM((1, c), F32),
        ],
        compiler_params=_params("arbitrary", "arbitrary"),
        name="rglru",
    )(gate, u, conv0_pad, h0.reshape(b, 1, c), cw, cb.reshape(1, c), gw, gb, lam.reshape(1, c))


def _cumsum_kernel(x_ref, o_ref, *, rows):
    s = x_ref.shape[1]
    r = lax.broadcasted_iota(jnp.int32, (rows, rows), 0)
    q = lax.broadcasted_iota(jnp.int32, (rows, rows), 1)
    tril = (q <= r).astype(F32)

    def block(bi, carry):
        r0 = pl.multiple_of(bi * rows, rows)
        cs = jnp.dot(tril, x_ref[0, pl.ds(r0, rows), :], preferred_element_type=F32,
                     precision=lax.Precision.HIGHEST) + carry
        o_ref[0, pl.ds(r0, rows), :] = cs
        return cs[rows - 1:rows, :]

    lax.fori_loop(0, s // rows, block, jnp.zeros((1, x_ref.shape[2]), F32))


def _cumsum(x, *, rows):
    b, s, w = x.shape
    assert s % rows == 0
    return pl.pallas_call(
        functools.partial(_cumsum_kernel, rows=rows),
        grid=(b,),
        in_specs=[pl.BlockSpec((1, s, w), lambda bi: (bi, 0, 0))],
        out_specs=pl.BlockSpec((1, s, w), lambda bi: (bi, 0, 0)),
        out_shape=jax.ShapeDtypeStruct((b, s, w), F32),
        compiler_params=_params("arbitrary"),
        name="cumsum",
    )(x)


def _split3(c):
    hi = c.astype(BF16)
    r1 = c - hi.astype(F32)
    mid = r1.astype(BF16)
    lo = (r1 - mid.astype(F32)).astype(BF16)
    return hi, mid, lo


def _aug_pieces(c, head):
    w = c.shape[1]
    src = lax.broadcasted_iota(jnp.int32, (w, V7X_LANES), 0)
    dst = lax.broadcasted_iota(jnp.int32, (w, V7X_LANES), 1)
    out = None
    for piece, q_lane, k_lane in zip(_split3(c * LOG2E), AUG_Q_LANES, AUG_K_LANES):
        sign = jnp.where(dst == q_lane, 1.0, jnp.where(dst == k_lane, -1.0, 0.0))
        route = jnp.where(src == head, sign, 0.0).astype(BF16)
        term = jnp.dot(piece, route, preferred_element_type=F32)
        out = term if out is None else out + term
    return out


def _aug_side(pieces, one_lanes):
    lane_id = lax.broadcasted_iota(jnp.int32, pieces.shape, 1)
    ones = (lane_id >= one_lanes[0]) & (lane_id <= one_lanes[-1])
    return jnp.where(ones, 1.0, pieces).astype(BF16)


def _attn_prompt_kernel(q_ref, k_ref, v_ref, g_ref, c_ref, o_ref,
                        qa_ref, ka_ref, qx_ref, vt_ref, st_ref, mt_ref, m_ref, acc_ref, *, tile):
    head = pl.program_id(1)
    qi = pl.program_id(2)
    n_kt = vt_ref.shape[0]
    dims = (((1,), (1,)), ((), ()))

    @pl.when(qi == 0)
    def _():
        def build(bi, carry):
            rows = pl.ds(pl.multiple_of(bi * tile, tile), tile)
            ka_ref[rows, 0:HEAD_DIM] = k_ref[rows, :]
            pieces = _aug_pieces(c_ref[0, rows, :], head)
            ka_ref[rows, HEAD_DIM:] = _aug_side(pieces, AUG_Q_LANES)
            qx_ref[rows, :] = _aug_side(pieces, AUG_K_LANES)
            vt_ref[bi, 0:HEAD_DIM, :] = v_ref[rows, :].astype(F32).T.astype(BF16)
            vt_ref[bi, HEAD_DIM:, :] = jnp.ones((ONES_ROWS, tile), BF16)
            return carry
        lax.fori_loop(0, n_kt, build, 0)

    q0 = pl.multiple_of(qi * (2 * tile), 2 * tile)
    qa_ref[:, 0:HEAD_DIM] = q_ref[0]
    qa_ref[:, HEAD_DIM:] = qx_ref[pl.ds(q0, 2 * tile), :]
    m_ref[...] = jnp.full(m_ref.shape, MASK_VALUE, F32)
    acc_ref[...] = jnp.zeros(acc_ref.shape, F32)

    def scores(half, kj, slot):
        k0 = pl.multiple_of(kj * tile, tile)
        st = lax.dot_general(ka_ref[pl.ds(k0, tile), :], qa_ref[half * tile:(half + 1) * tile, :], dims,
                             preferred_element_type=F32)
        st_ref[slot] = st
        mt_ref[slot] = jnp.max(st, axis=0, keepdims=True)

    def update(half, kj, slot, masked):
        st = st_ref[slot]
        if masked:
            key = lax.broadcasted_iota(jnp.int32, (tile, tile), 0)
            qry = lax.broadcasted_iota(jnp.int32, (tile, tile), 1)
            st = jnp.where(key <= qry, st, MASK_VALUE)
            m_tile = jnp.max(st, axis=0, keepdims=True)
        else:
            m_tile = mt_ref[slot]
        m_prev = m_ref[half]
        m_new = jnp.maximum(m_prev, m_tile)
        alpha = jnp.exp2(m_prev - m_new)
        p = jnp.exp2(st - m_new).astype(BF16)
        acc_ref[half] = alpha * acc_ref[half] + jnp.dot(vt_ref[kj], p, preferred_element_type=F32)
        m_ref[half] = m_new

    scores(0, 0, 0)

    def full_tiles(kj, carry):
        scores(1, kj, 1)
        update(0, kj, 0, False)
        scores(0, kj + 1, 0)
        update(1, kj, 1, False)
        return carry

    lax.fori_loop(0, 2 * qi, full_tiles, 0)
    scores(1, 2 * qi, 1)
    update(0, 2 * qi, 0, True)
    scores(1, 2 * qi + 1, 0)
    update(1, 2 * qi, 1, False)
    update(1, 2 * qi + 1, 0, True)
    for half in range(2):
        rows = slice(half * tile, (half + 1) * tile)
        o = (acc_ref[half, 0:HEAD_DIM, :] / acc_ref[half, HEAD_DIM:HEAD_DIM + 1, :]).T
        o_ref[0, rows, :] = (o * g_ref[0, rows, :]).astype(BF16)


def _attn_prompt(q, k, v, g, c, *, tile):
    b, s, d = q.shape
    nh = d // HEAD_DIM
    assert s % (2 * tile) == 0 and tile % V7X_LANES == 0 and k.shape == (nh, b * s, HEAD_DIM)
    qspec = pl.BlockSpec((1, 2 * tile, HEAD_DIM), lambda bi, hi, qi: (bi, qi, hi))
    kspec = pl.BlockSpec((None, s, HEAD_DIM), lambda bi, hi, qi: (hi, bi, 0))
    return pl.pallas_call(
        functools.partial(_attn_prompt_kernel, tile=tile),
        grid=(b, nh, s // (2 * tile)),
        in_specs=[qspec, kspec, kspec, qspec,
                  pl.BlockSpec((1, s, V7X_LANES), lambda bi, hi, qi: (bi, 0, 0))],
        out_specs=qspec,
        out_shape=jax.ShapeDtypeStruct((b, s, d), BF16),
        scratch_shapes=[
            pltpu.VMEM((2 * tile, 2 * HEAD_DIM), BF16),
            pltpu.VMEM((s, 2 * HEAD_DIM), BF16),
            pltpu.VMEM((s, HEAD_DIM), BF16),
            pltpu.VMEM((s // tile, HEAD_DIM + ONES_ROWS, tile), BF16),
            pltpu.VMEM((2, tile, tile), F32),
            pltpu.VMEM((2, 1, tile), F32),
            pltpu.VMEM((2, 1, tile), F32),
            pltpu.VMEM((2, HEAD_DIM + ONES_ROWS, tile), F32),
        ],
        compiler_params=_params("arbitrary", "arbitrary", "arbitrary"),
        name="attn_prompt",
    )(q, k, v, g, c)


def _attn_sample_kernel(q_ref, kc_ref, vc_ref, kn_ref, vn_ref, g_ref, c_ref, o_ref):
    t = q_ref.shape[1]
    nh = q_ref.shape[2] // HEAD_DIM
    past = kc_ref.shape[1] // nh
    dims = (((1,), (1,)), ((), ()))
    row = lax.broadcasted_iota(jnp.int32, (t, t), 0)
    col = lax.broadcasted_iota(jnp.int32, (t, t), 1)
    c_all = c_ref[0]
    c_past, c_new = c_all[0:past], c_all[past:past + t]
    def scores(head):
        cols = slice(head * HEAD_DIM, (head + 1) * HEAD_DIM)
        new_pieces = _aug_pieces(c_new, head)
        qa = jnp.concatenate([q_ref[0, :, cols], _aug_side(new_pieces, AUG_K_LANES)], axis=1)
        kca = jnp.concatenate([kc_ref[0, pl.ds(head, past, stride=nh), :].astype(BF16),
                               _aug_side(_aug_pieces(c_past, head), AUG_Q_LANES)], axis=1)
        kna = jnp.concatenate([kn_ref[0, :, cols].astype(BF16), _aug_side(new_pieces, AUG_Q_LANES)], axis=1)
        s_past = lax.dot_general(qa, kca, dims, preferred_element_type=F32)
        s_new = jnp.where(col <= row, lax.dot_general(qa, kna, dims, preferred_element_type=F32), MASK_VALUE)
        return s_past, s_new

    def finish(head, s_past, s_new):
        cols = slice(head * HEAD_DIM, (head + 1) * HEAD_DIM)
        m = jnp.maximum(jnp.max(s_past, axis=1, keepdims=True), jnp.max(s_new, axis=1, keepdims=True))
        p_past = jnp.exp2(s_past - m)
        p_new = jnp.exp2(s_new - m)
        l = jnp.sum(p_past, axis=1, keepdims=True) + jnp.sum(p_new, axis=1, keepdims=True)
        v_past = vc_ref[0, pl.ds(head, past, stride=nh), :].astype(BF16)
        o = (jnp.dot(p_past.astype(BF16), v_past, preferred_element_type=F32)
             + jnp.dot(p_new.astype(BF16), vn_ref[0, :, cols].astype(BF16), preferred_element_type=F32)) / l
        o_ref[0, :, cols] = (o * g_ref[0, :, cols]).astype(BF16)

    pending = scores(0)
    for head in range(nh):
        current, pending = pending, (scores(head + 1) if head + 1 < nh else None)
        finish(head, *current)


def _attn_sample(q, kc, vc, kn, vn, g, c):
    b, t, d = q.shape
    _, past, nh, hd = kc.shape
    assert nh % V7X_SUBLANES == 0
    kc, vc = kc.reshape(b, past * nh, hd), vc.reshape(b, past * nh, hd)
    new = pl.BlockSpec((1, t, d), lambda bi: (bi, 0, 0))
    old = pl.BlockSpec((1, past * nh, hd), lambda bi: (bi, 0, 0))
    return pl.pallas_call(
        _attn_sample_kernel,
        grid=(b,),
        in_specs=[new, old, old, new, new, new,
                  pl.BlockSpec((1, past + t, V7X_LANES), lambda bi: (bi, 0, 0))],
        out_specs=new,
        out_shape=jax.ShapeDtypeStruct((b, t, d), BF16),
        compiler_params=_params("arbitrary"),
        name="attn_sample",
    )(q, kc, vc, kn, vn, g, c)


FFN_TM, FFN_TF = 1024, 512
NORM_ROW_CHUNK, FFN_COL_CHUNK = 256, 512
PROJ_TM = 256
OUT_TM = 512
RGLRU_TT_PROMPT = 256
CUMSUM_ROWS_PROMPT, CUMSUM_ROWS_SAMPLE = 256, 272
ATTN_TILE = 512


def kernel(x_prompt, x_sample, state_conv, state_h, cache_k, cache_v, cache_logf, ffn_norm, ffn_w_in,
           ffn_w_out, a_norm, a_w_in, a_conv_w, a_conv_b, a_gate_w, a_gate_b, a_lambda, a_w_out, kv_norm,
           w_kv, w_f, b_f, b_norm, b_w_qg, b_w_o, final_norm):
    bp, sp, d = x_prompt.shape
    bs, ts, _ = x_sample.shape
    depth = ffn_w_in.shape[0]
    n_a = a_w_in.shape[0]
    n_heads = w_f.shape[1]

    ffn_order = [(l, half) for l in range(depth) for half in range(2)]
    ffn_w = {ffn_order[0]: (ffn_w_in[0, 0].astype(BF16), ffn_w_out[0, 0].astype(BF16))}
    a_w_in_b, a_gate_w_b, a_w_out_b = a_w_in.astype(BF16), a_gate_w.astype(BF16), a_w_out.astype(BF16)
    w_kv_b, b_w_qg_b, b_w_o_b = w_kv.astype(BF16), b_w_qg.astype(BF16), b_w_o.astype(BF16)
    wf_pad = jnp.pad(w_f, ((0, 0), (0, V7X_LANES - n_heads))).astype(BF16)
    bf_pad = jnp.pad(b_f, (0, V7X_LANES - n_heads)).reshape(1, V7X_LANES)

    streams = {"p": x_prompt.reshape(bp * sp, d), "s": x_sample.reshape(bs * ts, d)}
    batch = {"p": (bp, sp), "s": (bs, ts)}
    conv_out = {"p": [], "s": []}
    h_out = {"p": [], "s": []}
    kv = {}

    def ffn(x, l, half, name, apply_final=False):
        w_in_b, w_out_b = ffn_w[(l, half)]
        pos = ffn_order.index((l, half))
        cast_next = None
        if name == "p" and pos + 1 < len(ffn_order):
            cast_next = (ffn_w_in, ffn_w_out) + ffn_order[pos + 1]
        out = _ffn(x, ffn_norm[l, half], w_in_b, w_out_b, final_norm, apply_final=apply_final, tm=FFN_TM,
                   tf=FFN_TF, cast_next=cast_next)
        if cast_next is None:
            return out
        ffn_w[ffn_order[pos + 1]] = (out[1], out[2])
        return out[0]

    for l in range(depth):
        for name in ("p", "s"):
            x = ffn(streams[name], l, 0, name)
            b, t = batch[name]
            if l < n_a:
                gate, u = _proj2(x, a_norm[l], a_w_in_b[l], tm=PROJ_TM, a_epilogue="gelu")
                c_rnn = u.shape[1]
                if name == "p":
                    conv0 = jnp.zeros((b, CONV_W - 1, c_rnn), F32)
                    h0 = jnp.zeros((b, c_rnn), F32)
                    tt = RGLRU_TT_PROMPT
                else:
                    conv0, h0, tt = state_conv[l], state_h[l], t
                hg, nconv, nh = _rglru(gate.reshape(b, t, c_rnn), u.reshape(b, t, c_rnn), conv0, h0,
                                       a_conv_w[l], a_conv_b[l], a_gate_w_b[l], a_gate_b[l], a_lambda[l], tt=tt)
                conv_out[name].append(nconv)
                h_out[name].append(nh.reshape(b, c_rnn))
                x = _out_proj(hg.reshape(b * t, c_rnn), a_w_out_b[l], x, tm=OUT_TM)
            else:
                jb = l - n_a
                q, g = _proj2(x, b_norm[jb], b_w_qg_b[jb], tm=PROJ_TM, a_epilogue="attn_scale",
                              b_epilogue="sigmoid", a_dtype=BF16)
                k, v, f_pad = kv[name][:3]
                if name == "p":
                    k_heads, v_heads = kv[name][3:]
                    c = _cumsum(f_pad.reshape(b, t, V7X_LANES), rows=CUMSUM_ROWS_PROMPT)
                    og = _attn_prompt(q.reshape(b, t, d), k_heads, v_heads, g.reshape(b, t, d), c,
                                      tile=ATTN_TILE)
                else:
                    cache_f_pad = jnp.pad(cache_logf, ((0, 0), (0, 0), (0, V7X_LANES - n_heads)))
                    f_all = jnp.concatenate([cache_f_pad, f_pad.reshape(b, t, V7X_LANES)], axis=1)
                    c = _cumsum(f_all, rows=CUMSUM_ROWS_SAMPLE)
                    og = _attn_sample(q.reshape(b, t, d), cache_k, cache_v, k.reshape(b, t, d),
                                      v.reshape(b, t, d), g.reshape(b, t, d), c)
                x = _out_proj(og.reshape(b * t, d), b_w_o_b[jb], x, tm=OUT_TM)
            x = ffn(x, l, 1, name, apply_final=(l == depth - 1))
            if l == n_a - 1:
                kv[name] = _kv_proj(x, kv_norm, w_kv_b, wf_pad, bf_pad, tm=PROJ_TM, head_major=(name == "p"))
            streams[name] = x

    def finish(name):
        b, t = batch[name]
        k, v, f_pad = kv[name][:3]
        return (streams[name].reshape(b, t, d), jnp.stack(conv_out[name]), jnp.stack(h_out[name]),
                k.reshape(b, t, n_heads, HEAD_DIM), v.reshape(b, t, n_heads, HEAD_DIM),
                f_pad[:, :n_heads].reshape(b, t, n_heads))

    yp, convp, hp, kp, vp, fp = finish("p")
    ys, convs, hs, ks, vs, fs = finish("s")
    return (yp, ys, convp, hp, kp, vp, fp, convs, hs, ks, vs, fs)
```

```python
import functools

import jax
import jax.numpy as jnp
from jax import lax
from jax.experimental import pallas as pl
from jax.experimental.pallas import tpu as pltpu

F32 = jnp.float32
BF16 = jnp.bfloat16

RMS_EPS = 1e-6
LRU_C = 8.0
HEAD_DIM = 128
ATTN_SCALE = HEAD_DIM ** -0.5
LOG2E = 1.4426950408889634
MASK_VALUE = -1e30

V7X_LANES = 128
V7X_SUBLANES = 8
V7X_VMEM_BYTES = 64 * 1024 * 1024
VMEM_LIMIT_BYTES = V7X_VMEM_BYTES - 5 * 1024 * 1024

RNN_BLOCK = 256
SCAN_GROUP = 8
CONV_W = 4
CONV_PAD_ROWS = V7X_SUBLANES

AUG_Q_LANES = (0, 1, 2)
AUG_K_LANES = (3, 4, 5)
ONES_ROWS = 16


def _params(*semantics):
    return pltpu.CompilerParams(dimension_semantics=semantics, vmem_limit_bytes=VMEM_LIMIT_BYTES)


def _rms(x, g):
    y = x * lax.rsqrt(jnp.mean(x * x, axis=-1, keepdims=True) + RMS_EPS)
    return y * g


def _ffn_kernel(x_ref, g_ref, wg_ref, wu_ref, wo_ref, fin_ref, *rest, apply_final, cast_next, row_chunk,
                col_chunk):
    if cast_next:
        src_in_ref, src_out_ref, o_ref, dst_in_ref, dst_out_ref, xn_ref = rest
        dst_in_ref[...] = src_in_ref[...].astype(BF16)
        dst_out_ref[...] = src_out_ref[...].astype(BF16)
    else:
        o_ref, xn_ref = rest
    j = pl.program_id(1)
    tm, d = x_ref.shape

    def by_rows(fn):
        def body(ri, carry):
            fn(pl.ds(pl.multiple_of(ri * row_chunk, row_chunk), row_chunk))
            return carry
        lax.fori_loop(0, tm // row_chunk, body, 0)

    @pl.when(j == 0)
    def _():
        def prep(rows):
            x = x_ref[rows, :]
            xn_ref[rows, :] = _rms(x, g_ref[...]).astype(BF16)
            o_ref[rows, :] = x
        by_rows(prep)

    xn = xn_ref[...]
    gate = jnp.dot(xn, wg_ref[...], preferred_element_type=F32)
    up = jnp.dot(xn, wu_ref[...], preferred_element_type=F32)
    h = (0.5 * (gate * jax.nn.sigmoid(gate)) * up).astype(BF16)
    for c in range(d // col_chunk):
        cols = slice(c * col_chunk, (c + 1) * col_chunk)
        o_ref[:, cols] += jnp.dot(h, wo_ref[:, cols], preferred_element_type=F32)

    if apply_final:
        @pl.when(j == pl.num_programs(1) - 1)
        def _():
            def norm(rows):
                o_ref[rows, :] = _rms(o_ref[rows, :], fin_ref[...])
            by_rows(norm)


def _ffn(x, g, w_in, w_out, fin, *, apply_final, tm, tf, cast_next=None):
    n, d = x.shape
    dff = w_out.shape[0]
    ni, nj = n // tm, dff // tf
    assert n % tm == 0 and dff % tf == 0 and w_in.shape == (d, 2 * dff)
    row_chunk, col_chunk = min(NORM_ROW_CHUNK, tm), min(FFN_COL_CHUNK, d)
    assert tm % row_chunk == 0 and d % col_chunk == 0
    in_specs = [
        pl.BlockSpec((tm, d), lambda i, j: (i, 0)),
        pl.BlockSpec((1, d), lambda i, j: (0, 0)),
        pl.BlockSpec((d, tf), lambda i, j: (0, j)),
        pl.BlockSpec((d, tf), lambda i, j: (0, j + nj)),
        pl.BlockSpec((tf, d), lambda i, j: (j, 0)),
        pl.BlockSpec((1, d), lambda i, j: (0, 0)),
    ]
    out_specs = [pl.BlockSpec((tm, d), lambda i, j: (i, 0))]
    out_shape = [jax.ShapeDtypeStruct((n, d), F32)]
    operands = [x, g.reshape(1, d), w_in, w_in, w_out, fin.reshape(1, d)]
    if cast_next is not None:
        src_in, src_out, layer, half = cast_next
        bi, bo = (d // ni, 2 * dff // nj), (dff // nj, d // ni)
        assert d % ni == 0 and bi[0] % (2 * V7X_SUBLANES) == 0 and bi[1] % V7X_LANES == 0
        assert bo[0] % (2 * V7X_SUBLANES) == 0 and bo[1] % V7X_LANES == 0
        in_specs += [pl.BlockSpec((None, None) + bi, lambda i, j: (layer, half, i, j)),
                     pl.BlockSpec((None, None) + bo, lambda i, j: (layer, half, j, i))]
        out_specs += [pl.BlockSpec(bi, lambda i, j: (i, j)), pl.BlockSpec(bo, lambda i, j: (j, i))]
        out_shape += [jax.ShapeDtypeStruct((d, 2 * dff), BF16), jax.ShapeDtypeStruct((dff, d), BF16)]
        operands += [src_in, src_out]
    outs = pl.pallas_call(
        functools.partial(_ffn_kernel, apply_final=apply_final, cast_next=cast_next is not None,
                          row_chunk=row_chunk, col_chunk=col_chunk),
        grid=(ni, nj),
        in_specs=in_specs,
        out_specs=out_specs,
        out_shape=out_shape,
        scratch_shapes=[pltpu.VMEM((tm, d), BF16)],
        compiler_params=_params("arbitrary", "arbitrary"),
        name="ffn",
    )(*operands)
    return outs if cast_next is not None else outs[0]


def _gelu_tanh(x):
    return 0.5 * x * (1.0 + jnp.tanh(0.7978845608028654 * (x + 0.044715 * (x * x * x))))


_EPILOGUES = {
    None: lambda y: y,
    "gelu": _gelu_tanh,
    "sigmoid": jax.nn.sigmoid,
    "attn_scale": lambda y: y * (ATTN_SCALE * LOG2E),
}


def _proj2_kernel(x_ref, g_ref, w_ref, oa_ref, ob_ref, *, a_epilogue, b_epilogue):
    half = w_ref.shape[1] // 2
    xn = _rms(x_ref[...], g_ref[...]).astype(BF16)
    a = jnp.dot(xn, w_ref[:, 0:half], preferred_element_type=F32)
    oa_ref[...] = _EPILOGUES[a_epilogue](a).astype(oa_ref.dtype)
    b = jnp.dot(xn, w_ref[:, half:], preferred_element_type=F32)
    ob_ref[...] = _EPILOGUES[b_epilogue](b).astype(ob_ref.dtype)


def _proj2(x, g, w, *, tm, a_epilogue=None, b_epilogue=None, a_dtype=F32):
    n, d = x.shape
    half = w.shape[1] // 2
    assert n % tm == 0
    return pl.pallas_call(
        functools.partial(_proj2_kernel, a_epilogue=a_epilogue, b_epilogue=b_epilogue),
        grid=(n // tm,),
        in_specs=[
            pl.BlockSpec((tm, d), lambda i: (i, 0)),
            pl.BlockSpec((1, d), lambda i: (0, 0)),
            pl.BlockSpec(w.shape, lambda i: (0, 0), pipeline_mode=pl.Buffered(1)),
        ],
        out_specs=[pl.BlockSpec((tm, half), lambda i: (i, 0))] * 2,
        out_shape=[jax.ShapeDtypeStruct((n, half), a_dtype), jax.ShapeDtypeStruct((n, half), F32)],
        compiler_params=_params("arbitrary"),
        name="proj2",
    )(x, g.reshape(1, d), w)


def _log_sigmoid(z):
    return jnp.minimum(z, 0.0) - jnp.log1p(jnp.exp(-jnp.abs(z)))


def _kv_kernel(x_ref, g_ref, w_ref, wf_ref, bf_ref, k_ref, v_ref, f_ref, *head_major_refs):
    half = w_ref.shape[1] // 2
    xn = _rms(x_ref[...], g_ref[...]).astype(BF16)
    k = jnp.dot(xn, w_ref[:, 0:half], preferred_element_type=F32)
    v = jnp.dot(xn, w_ref[:, half:], preferred_element_type=F32)
    k_ref[...] = k
    v_ref[...] = v
    z = jnp.dot(xn, wf_ref[...], preferred_element_type=F32) + bf_ref[...]
    f_ref[...] = _log_sigmoid(z)
    if head_major_refs:
        kh_ref, vh_ref = head_major_refs
        for head in range(half // HEAD_DIM):
            cols = slice(head * HEAD_DIM, (head + 1) * HEAD_DIM)
            kh_ref[head] = k[:, cols].astype(BF16)
            vh_ref[head] = v[:, cols].astype(BF16)


def _kv_proj(x, g, w_kv, wf_pad, bf_pad, *, tm, head_major):
    n, d = x.shape
    half = w_kv.shape[1] // 2
    nh = half // HEAD_DIM
    assert n % tm == 0
    out = jax.ShapeDtypeStruct((n, half), F32)
    out_specs = [
        pl.BlockSpec((tm, half), lambda i: (i, 0)),
        pl.BlockSpec((tm, half), lambda i: (i, 0)),
        pl.BlockSpec((tm, V7X_LANES), lambda i: (i, 0)),
    ]
    out_shape = [out, out, jax.ShapeDtypeStruct((n, V7X_LANES), F32)]
    if head_major:
        out_specs += [pl.BlockSpec((nh, tm, HEAD_DIM), lambda i: (0, i, 0))] * 2
        out_shape += [jax.ShapeDtypeStruct((nh, n, HEAD_DIM), BF16)] * 2
    return pl.pallas_call(
        _kv_kernel,
        grid=(n // tm,),
        in_specs=[
            pl.BlockSpec((tm, d), lambda i: (i, 0)),
            pl.BlockSpec((1, d), lambda i: (0, 0)),
            pl.BlockSpec(w_kv.shape, lambda i: (0, 0), pipeline_mode=pl.Buffered(1)),
            pl.BlockSpec((d, V7X_LANES), lambda i: (0, 0)),
            pl.BlockSpec((1, V7X_LANES), lambda i: (0, 0)),
        ],
        out_specs=out_specs,
        out_shape=out_shape,
        compiler_params=_params("arbitrary"),
        name="kv_proj",
    )(x, g.reshape(1, d), w_kv, wf_pad, bf_pad)


def _out_proj_kernel(a_ref, w_ref, x_ref, o_ref):
    o_ref[...] = x_ref[...] + jnp.dot(a_ref[...], w_ref[...], preferred_element_type=F32)


def _out_proj(a, w, x, *, tm):
    n, k = a.shape
    d = w.shape[1]
    assert n % tm == 0
    return pl.pallas_call(
        _out_proj_kernel,
        grid=(n // tm,),
        in_specs=[
            pl.BlockSpec((tm, k), lambda i: (i, 0)),
            pl.BlockSpec((k, d), lambda i: (0, 0)),
            pl.BlockSpec((tm, d), lambda i: (i, 0)),
        ],
        out_specs=pl.BlockSpec((tm, d), lambda i: (i, 0)),
        out_shape=jax.ShapeDtypeStruct((n, d), F32),
        compiler_params=_params("arbitrary"),
        name="out_proj",
    )(a, w, x)


def _softplus(y):
    return jnp.maximum(y, 0.0) + jnp.log1p(jnp.exp(-jnp.abs(y)))


def _scan_rows(a_ref, b_ref, hin_ref, h_in):
    g = SCAN_GROUP
    groups = a_ref.shape[0] // g

    def phase(k):
        return pl.ds(k, groups, stride=g)

    a_acc, b_acc = a_ref[phase(0), :], b_ref[phase(0), :]
    for k in range(1, g):
        ak = a_ref[phase(k), :]
        b_acc = ak * b_acc + b_ref[phase(k), :]
        a_acc = ak * a_acc
        a_ref[phase(k), :] = a_acc
        b_ref[phase(k), :] = b_acc
    hc = h_in
    for gi in range(groups):
        hin_ref[gi:gi + 1, :] = hc
        hc = a_acc[gi:gi + 1, :] * hc + b_acc[gi:gi + 1, :]
    hin = hin_ref[...]
    for k in range(g):
        b_ref[phase(k), :] = a_ref[phase(k), :] * hin + b_ref[phase(k), :]
    return hc


def _rglru_kernel(gate_ref, u_ref, conv0_ref, h0_ref, cw_ref, cb_ref, gw_ref, gb_ref, lam_ref,
                  hg_ref, nconv_ref, nh_ref, ubuf, a_s, b_s, hin_s, hcar, *, tt):
    c = u_ref.shape[-1]
    pad = CONV_PAD_ROWS

    @pl.when(pl.program_id(1) == 0)
    def _():
        ubuf[0:pad, :] = conv0_ref[0]
        hcar[...] = h0_ref[0]

    ubuf[pad:pad + tt, :] = u_ref[0]

    for blk in range(c // RNN_BLOCK):
        cols = slice(blk * RNN_BLOCK, (blk + 1) * RNN_BLOCK)
        uc = cb_ref[:, cols]
        for k in range(CONV_W):
            uc = uc + ubuf[pad - (CONV_W - 1 - k):pad - (CONV_W - 1 - k) + tt, cols] * cw_ref[k:k + 1, cols]
        ub = uc.astype(BF16)
        r = jax.nn.sigmoid(jnp.dot(ub, gw_ref[0, blk], preferred_element_type=F32) + gb_ref[0:1, cols])
        i = jax.nn.sigmoid(jnp.dot(ub, gw_ref[1, blk], preferred_element_type=F32) + gb_ref[1:2, cols])
        log_a = (-LRU_C * r) * _softplus(-lam_ref[:, cols])
        a = jnp.exp(log_a)
        inp = jnp.sqrt(1.0 - a * a) * (i * uc)
        for j in range(RNN_BLOCK // V7X_LANES):
            slab = blk * (RNN_BLOCK // V7X_LANES) + j
            lanes = slice(slab * V7X_LANES, (slab + 1) * V7X_LANES)
            a_s[slab] = a[:, j * V7X_LANES:(j + 1) * V7X_LANES]
            b_s[slab] = inp[:, j * V7X_LANES:(j + 1) * V7X_LANES]
            hcar[:, lanes] = _scan_rows(a_s.at[slab], b_s.at[slab], hin_s.at[slab], hcar[:, lanes])
            hg_ref[0, :, lanes] = (b_s[slab] * gate_ref[0, :, lanes]).astype(BF16)

    tail = ubuf[pad + tt - (CONV_W - 1):pad + tt, :]
    nconv_ref[0] = tail
    ubuf[pad - (CONV_W - 1):pad, :] = tail
    nh_ref[0] = hcar[...]


def _rglru(gate, u, conv0, h0, cw, cb, gw, gb, lam, *, tt):
    b, t, c = u.shape
    assert t % tt == 0 and tt % V7X_SUBLANES == 0 and tt >= CONV_W - 1
    nblk = c // RNN_BLOCK
    conv0_pad = jnp.pad(conv0, ((0, 0), (CONV_PAD_ROWS - (CONV_W - 1), 0), (0, 0)))
    return pl.pallas_call(
        functools.partial(_rglru_kernel, tt=tt),
        grid=(b, t // tt),
        in_specs=[
            pl.BlockSpec((1, tt, c), lambda bi, ti: (bi, ti, 0)),
            pl.BlockSpec((1, tt, c), lambda bi, ti: (bi, ti, 0)),
            pl.BlockSpec((1, CONV_PAD_ROWS, c), lambda bi, ti: (bi, 0, 0)),
            pl.BlockSpec((1, 1, c), lambda bi, ti: (bi, 0, 0)),
            pl.BlockSpec((CONV_W, c), lambda bi, ti: (0, 0)),
            pl.BlockSpec((1, c), lambda bi, ti: (0, 0)),
            pl.BlockSpec((2, nblk, RNN_BLOCK, RNN_BLOCK), lambda bi, ti: (0, 0, 0, 0)),
            pl.BlockSpec((2, c), lambda bi, ti: (0, 0)),
            pl.BlockSpec((1, c), lambda bi, ti: (0, 0)),
        ],
        out_specs=[
            pl.BlockSpec((1, tt, c), lambda bi, ti: (bi, ti, 0)),
            pl.BlockSpec((1, CONV_W - 1, c), lambda bi, ti: (bi, 0, 0)),
            pl.BlockSpec((1, 1, c), lambda bi, ti: (bi, 0, 0)),
        ],
        out_shape=[
            jax.ShapeDtypeStruct((b, t, c), BF16),
            jax.ShapeDtypeStruct((b, CONV_W - 1, c), F32),
            jax.ShapeDtypeStruct((b, 1, c), F32),
        ],
        scratch_shapes=[
            pltpu.VMEM((CONV_PAD_ROWS + tt, c), F32),
            pltpu.VMEM((c // V7X_LANES, tt, V7X_LANES), F32),
            pltpu.VMEM((c // V7X_LANES, tt, V7X_LANES), F32),
            pltpu.VMEM((c // V7X_LANES, tt // SCAN_GROUP, V7X_LANES), F32),
            pltpu.VMEM((1, c), F32),
        ],
        compiler_params=_params("arbitrary", "arbitrary"),
        name="rglru",
    )(gate, u, conv0_pad, h0.reshape(b, 1, c), cw, cb.reshape(1, c), gw, gb, lam.reshape(1, c))


def _cumsum_kernel(x_ref, o_ref, *, rows):
    s = x_ref.shape[1]
    r = lax.broadcasted_iota(jnp.int32, (rows, rows), 0)
    q = lax.broadcasted_iota(jnp.int32, (rows, rows), 1)
    tril = (q <= r).astype(F32)

    def block(bi, carry):
        r0 = pl.multiple_of(bi * rows, rows)
        cs = jnp.dot(tril, x_ref[0, pl.ds(r0, rows), :], preferred_element_type=F32,
                     precision=lax.Precision.HIGHEST) + carry
        o_ref[0, pl.ds(r0, rows), :] = cs
        return cs[rows - 1:rows, :]

    lax.fori_loop(0, s // rows, block, jnp.zeros((1, x_ref.shape[2]), F32))


def _cumsum(x, *, rows):
    b, s, w = x.shape
    assert s % rows == 0
    return pl.pallas_call(
        functools.partial(_cumsum_kernel, rows=rows),
        grid=(b,),
        in_specs=[pl.BlockSpec((1, s, w), lambda bi: (bi, 0, 0))],
        out_specs=pl.BlockSpec((1, s, w), lambda bi: (bi, 0, 0)),
        out_shape=jax.ShapeDtypeStruct((b, s, w), F32),
        compiler_params=_params("arbitrary"),
        name="cumsum",
    )(x)


def _split3(c):
    hi = c.astype(BF16)
    r1 = c - hi.astype(F32)
    mid = r1.astype(BF16)
    lo = (r1 - mid.astype(F32)).astype(BF16)
    return hi, mid, lo


def _aug_pieces(c, head):
    w = c.shape[1]
    src = lax.broadcasted_iota(jnp.int32, (w, V7X_LANES), 0)
    dst = lax.broadcasted_iota(jnp.int32, (w, V7X_LANES), 1)
    out = None
    for piece, q_lane, k_lane in zip(_split3(c * LOG2E), AUG_Q_LANES, AUG_K_LANES):
        sign = jnp.where(dst == q_lane, 1.0, jnp.where(dst == k_lane, -1.0, 0.0))
        route = jnp.where(src == head, sign, 0.0).astype(BF16)
        term = jnp.dot(piece, route, preferred_element_type=F32)
        out = term if out is None else out + term
    return out


def _aug_side(pieces, one_lanes):
    lane_id = lax.broadcasted_iota(jnp.int32, pieces.shape, 1)
    ones = (lane_id >= one_lanes[0]) & (lane_id <= one_lanes[-1])
    return jnp.where(ones, 1.0, pieces).astype(BF16)


def _attn_prompt_kernel(q_ref, k_ref, v_ref, g_ref, c_ref, o_ref,
                        qa_ref, ka_ref, qx_ref, vt_ref, st_ref, mt_ref, m_ref, acc_ref, *, tile):
    head = pl.program_id(1)
    qi = pl.program_id(2)
    n_kt = vt_ref.shape[0]
    dims = (((1,), (1,)), ((), ()))

    @pl.when(qi == 0)
    def _():
        def build(bi, carry):
            rows = pl.ds(pl.multiple_of(bi * tile, tile), tile)
            ka_ref[rows, 0:HEAD_DIM] = k_ref[rows, :]
            pieces = _aug_pieces(c_ref[0, rows, :], head)
            ka_ref[rows, HEAD_DIM:] = _aug_side(pieces, AUG_Q_LANES)
            qx_ref[rows, :] = _aug_side(pieces, AUG_K_LANES)
            vt_ref[bi, 0:HEAD_DIM, :] = v_ref[rows, :].astype(F32).T.astype(BF16)
            vt_ref[bi, HEAD_DIM:, :] = jnp.ones((ONES_ROWS, tile), BF16)
            return carry
        lax.fori_loop(0, n_kt, build, 0)

    q0 = pl.multiple_of(qi * (2 * tile), 2 * tile)
    qa_ref[:, 0:HEAD_DIM] = q_ref[0]
    qa_ref[:, HEAD_DIM:] = qx_ref[pl.ds(q0, 2 * tile), :]
    m_ref[...] = jnp.full(m_ref.shape, MASK_VALUE, F32)
    acc_ref[...] = jnp.zeros(acc_ref.shape, F32)

    def scores(half, kj, slot):
        k0 = pl.multiple_of(kj * tile, tile)
        st = lax.dot_general(ka_ref[pl.ds(k0, tile), :], qa_ref[half * tile:(half + 1) * tile, :], dims,
                             preferred_element_type=F32)
        st_ref[slot] = st
        mt_ref[slot] = jnp.max(st, axis=0, keepdims=True)

    def update(half, kj, slot, masked):
        st = st_ref[slot]
        if masked:
            key = lax.broadcasted_iota(jnp.int32, (tile, tile), 0)
            qry = lax.broadcasted_iota(jnp.int32, (tile, tile), 1)
            st = jnp.where(key <= qry, st, MASK_VALUE)
            m_tile = jnp.max(st, axis=0, keepdims=True)
        else:
            m_tile = mt_ref[slot]
        m_prev = m_ref[half]
        m_new = jnp.maximum(m_prev, m_tile)
        alpha = jnp.exp2(m_prev - m_new)
        p = jnp.exp2(st - m_new).astype(BF16)
        acc_ref[half] = alpha * acc_ref[half] + jnp.dot(vt_ref[kj], p, preferred_element_type=F32)
        m_ref[half] = m_new

    scores(0, 0, 0)
    scores(1, 0, 2)

    def two_key_tiles(i, carry):
        kj = 2 * i
        scores(0, kj + 1, 1)
        scores(1, kj + 1, 3)
        update(0, kj, 0, False)
        update(1, kj, 2, False)
        scores(0, kj + 2, 0)
        scores(1, kj + 2, 2)
        update(0, kj + 1, 1, False)
        update(1, kj + 1, 3, False)
        return carry

    lax.fori_loop(0, qi, two_key_tiles, 0)
    scores(1, 2 * qi + 1, 3)
    update(0, 2 * qi, 0, True)
    update(1, 2 * qi, 2, False)
    update(1, 2 * qi + 1, 3, True)
    for half in range(2):
        rows = slice(half * tile, (half + 1) * tile)
        o = (acc_ref[half, 0:HEAD_DIM, :] / acc_ref[half, HEAD_DIM:HEAD_DIM + 1, :]).T
        o_ref[0, rows, :] = (o * g_ref[0, rows, :]).astype(BF16)


def _attn_prompt(q, k, v, g, c, *, tile):
    b, s, d = q.shape
    nh = d // HEAD_DIM
    assert s % (2 * tile) == 0 and tile % V7X_LANES == 0 and k.shape == (nh, b * s, HEAD_DIM)
    qspec = pl.BlockSpec((1, 2 * tile, HEAD_DIM), lambda bi, hi, qi: (bi, qi, hi))
    kspec = pl.BlockSpec((None, s, HEAD_DIM), lambda bi, hi, qi: (hi, bi, 0))
    return pl.pallas_call(
        functools.partial(_attn_prompt_kernel, tile=tile),
        grid=(b, nh, s // (2 * tile)),
        in_specs=[qspec, kspec, kspec, qspec,
                  pl.BlockSpec((1, s, V7X_LANES), lambda bi, hi, qi: (bi, 0, 0))],
        out_specs=qspec,
        out_shape=jax.ShapeDtypeStruct((b, s, d), BF16),
        scratch_shapes=[
            pltpu.VMEM((2 * tile, 2 * HEAD_DIM), BF16),
            pltpu.VMEM((s, 2 * HEAD_DIM), BF16),
            pltpu.VMEM((s, HEAD_DIM), BF16),
            pltpu.VMEM((s // tile, HEAD_DIM + ONES_ROWS, tile), BF16),
            pltpu.VMEM((4, tile, tile), F32),
            pltpu.VMEM((4, 1, tile), F32),
            pltpu.VMEM((2, 1, tile), F32),
            pltpu.VMEM((2, HEAD_DIM + ONES_ROWS, tile), F32),
        ],
        compiler_params=_params("arbitrary", "arbitrary", "arbitrary"),
        name="attn_prompt",
    )(q, k, v, g, c)


def _attn_sample_kernel(q_ref, kc_ref, vc_ref, kn_ref, vn_ref, g_ref, c_ref, o_ref):
    t = q_ref.shape[1]
    nh = q_ref.shape[2] // HEAD_DIM
    past = kc_ref.shape[1] // nh
    dims = (((1,), (1,)), ((), ()))
    row = lax.broadcasted_iota(jnp.int32, (t, t), 0)
    col = lax.broadcasted_iota(jnp.int32, (t, t), 1)
    c_all = c_ref[0]
    c_past, c_new = c_all[0:past], c_all[past:past + t]
    def scores(head):
        cols = slice(head * HEAD_DIM, (head + 1) * HEAD_DIM)
        new_pieces = _aug_pieces(c_new, head)
        qa = jnp.concatenate([q_ref[0, :, cols], _aug_side(new_pieces, AUG_K_LANES)], axis=1)
        kca = jnp.concatenate([kc_ref[0, pl.ds(head, past, stride=nh), :].astype(BF16),
                               _aug_side(_aug_pieces(c_past, head), AUG_Q_LANES)], axis=1)
        kna = jnp.concatenate([kn_ref[0, :, cols].astype(BF16), _aug_side(new_pieces, AUG_Q_LANES)], axis=1)
        s_past = lax.dot_general(qa, kca, dims, preferred_element_type=F32)
        s_new = jnp.where(col <= row, lax.dot_general(qa, kna, dims, preferred_element_type=F32), MASK_VALUE)
        return s_past, s_new

    def finish(head, s_past, s_new):
        cols = slice(head * HEAD_DIM, (head + 1) * HEAD_DIM)
        m = jnp.maximum(jnp.max(s_past, axis=1, keepdims=True), jnp.max(s_new, axis=1, keepdims=True))
        p_past = jnp.exp2(s_past - m)
        p_new = jnp.exp2(s_new - m)
        l = jnp.sum(p_past, axis=1, keepdims=True) + jnp.sum(p_new, axis=1, keepdims=True)
        v_past = vc_ref[0, pl.ds(head, past, stride=nh), :].astype(BF16)
        o = (jnp.dot(p_past.astype(BF16), v_past, preferred_element_type=F32)
             + jnp.dot(p_new.astype(BF16), vn_ref[0, :, cols].astype(BF16), preferred_element_type=F32)) / l
        o_ref[0, :, cols] = (o * g_ref[0, :, cols]).astype(BF16)

    pending = scores(0)
    for head in range(nh):
        current, pending = pending, (scores(head + 1) if head + 1 < nh else None)
        finish(head, *current)


def _attn_sample(q, kc, vc, kn, vn, g, c):
    b, t, d = q.shape
    _, past, nh, hd = kc.shape
    assert nh % V7X_SUBLANES == 0
    kc, vc = kc.reshape(b, past * nh, hd), vc.reshape(b, past * nh, hd)
    new = pl.BlockSpec((1, t, d), lambda bi: (bi, 0, 0))
    old = pl.BlockSpec((1, past * nh, hd), lambda bi: (bi, 0, 0))
    return pl.pallas_call(
        _attn_sample_kernel,
        grid=(b,),
        in_specs=[new, old, old, new, new, new,
                  pl.BlockSpec((1, past + t, V7X_LANES), lambda bi: (bi, 0, 0))],
        out_specs=new,
        out_shape=jax.ShapeDtypeStruct((b, t, d), BF16),
        compiler_params=_params("arbitrary"),
        name="attn_sample",
    )(q, kc, vc, kn, vn, g, c)


FFN_TM, FFN_TF = 1024, 512
NORM_ROW_CHUNK, FFN_COL_CHUNK = 256, 512
PROJ_TM = 256
OUT_TM = 512
RGLRU_TT_PROMPT = 256
CUMSUM_ROWS_PROMPT, CUMSUM_ROWS_SAMPLE = 256, 272
ATTN_TILE = 512


def kernel(x_prompt, x_sample, state_conv, state_h, cache_k, cache_v, cache_logf, ffn_norm, ffn_w_in,
           ffn_w_out, a_norm, a_w_in, a_conv_w, a_conv_b, a_gate_w, a_gate_b, a_lambda, a_w_out, kv_norm,
           w_kv, w_f, b_f, b_norm, b_w_qg, b_w_o, final_norm):
    bp, sp, d = x_prompt.shape
    bs, ts, _ = x_sample.shape
    depth = ffn_w_in.shape[0]
    n_a = a_w_in.shape[0]
    n_heads = w_f.shape[1]

    ffn_order = [(l, half) for l in range(depth) for half in range(2)]
    ffn_w = {ffn_order[0]: (ffn_w_in[0, 0].astype(BF16), ffn_w_out[0, 0].astype(BF16))}
    a_w_in_b, a_gate_w_b, a_w_out_b = a_w_in.astype(BF16), a_gate_w.astype(BF16), a_w_out.astype(BF16)
    w_kv_b, b_w_qg_b, b_w_o_b = w_kv.astype(BF16), b_w_qg.astype(BF16), b_w_o.astype(BF16)
    wf_pad = jnp.pad(w_f, ((0, 0), (0, V7X_LANES - n_heads))).astype(BF16)
    bf_pad = jnp.pad(b_f, (0, V7X_LANES - n_heads)).reshape(1, V7X_LANES)

    streams = {"p": x_prompt.reshape(bp * sp, d), "s": x_sample.reshape(bs * ts, d)}
    batch = {"p": (bp, sp), "s": (bs, ts)}
    conv_out = {"p": [], "s": []}
    h_out = {"p": [], "s": []}
    kv = {}

    def ffn(x, l, half, name, apply_final=False):
        w_in_b, w_out_b = ffn_w[(l, half)]
        pos = ffn_order.index((l, half))
        cast_next = None
        if name == "p" and pos + 1 < len(ffn_order):
            cast_next = (ffn_w_in, ffn_w_out) + ffn_order[pos + 1]
        out = _ffn(x, ffn_norm[l, half], w_in_b, w_out_b, final_norm, apply_final=apply_final, tm=FFN_TM,
                   tf=FFN_TF, cast_next=cast_next)
        if cast_next is None:
            return out
        ffn_w[ffn_order[pos + 1]] = (out[1], out[2])
        return out[0]

    for l in range(depth):
        for name in ("p", "s"):
            x = ffn(streams[name], l, 0, name)
            b, t = batch[name]
            if l < n_a:
                gate, u = _proj2(x, a_norm[l], a_w_in_b[l], tm=PROJ_TM, a_epilogue="gelu")
                c_rnn = u.shape[1]
                if name == "p":
                    conv0 = jnp.zeros((b, CONV_W - 1, c_rnn), F32)
                    h0 = jnp.zeros((b, c_rnn), F32)
                    tt = RGLRU_TT_PROMPT
                else:
                    conv0, h0, tt = state_conv[l], state_h[l], t
                hg, nconv, nh = _rglru(gate.reshape(b, t, c_rnn), u.reshape(b, t, c_rnn), conv0, h0,
                                       a_conv_w[l], a_conv_b[l], a_gate_w_b[l], a_gate_b[l], a_lambda[l], tt=tt)
                conv_out[name].append(nconv)
                h_out[name].append(nh.reshape(b, c_rnn))
                x = _out_proj(hg.reshape(b * t, c_rnn), a_w_out_b[l], x, tm=OUT_TM)
            else:
                jb = l - n_a
                q, g = _proj2(x, b_norm[jb], b_w_qg_b[jb], tm=PROJ_TM, a_epilogue="attn_scale",
                              b_epilogue="sigmoid", a_dtype=BF16)
                k, v, f_pad = kv[name][:3]
                if name == "p":
                    k_heads, v_heads = kv[name][3:]
                    c = _cumsum(f_pad.reshape(b, t, V7X_LANES), rows=CUMSUM_ROWS_PROMPT)
                    og = _attn_prompt(q.reshape(b, t, d), k_heads, v_heads, g.reshape(b, t, d), c,
                                      tile=ATTN_TILE)
                else:
                    cache_f_pad = jnp.pad(cache_logf, ((0, 0), (0, 0), (0, V7X_LANES - n_heads)))
                    f_all = jnp.concatenate([cache_f_pad, f_pad.reshape(b, t, V7X_LANES)], axis=1)
                    c = _cumsum(f_all, rows=CUMSUM_ROWS_SAMPLE)
                    og = _attn_sample(q.reshape(b, t, d), cache_k, cache_v, k.reshape(b, t, d),
                                      v.reshape(b, t, d), g.reshape(b, t, d), c)
                x = _out_proj(og.reshape(b * t, d), b_w_o_b[jb], x, tm=OUT_TM)
            x = ffn(x, l, 1, name, apply_final=(l == depth - 1))
            if l == n_a - 1:
                kv[name] = _kv_proj(x, kv_norm, w_kv_b, wf_pad, bf_pad, tm=PROJ_TM, head_major=(name == "p"))
            streams[name] = x

    def finish(name):
        b, t = batch[name]
        k, v, f_pad = kv[name][:3]
        return (streams[name].reshape(b, t, d), jnp.stack(conv_out[name]), jnp.stack(h_out[name]),
                k.reshape(b, t, n_heads, HEAD_DIM), v.reshape(b, t, n_heads, HEAD_DIM),
                f_pad[:, :n_heads].reshape(b, t, n_heads))

    yp, convp, hp, kp, vp, fp = finish("p")
    ys, convs, hs, ks, vs, fs = finish("s")
    return (yp, ys, convp, hp, kp, vp, fp, convs, hs, ks, vs, fs)
```

```python
import functools

import jax
import jax.numpy as jnp
from jax import lax
from jax.experimental import pallas as pl
from jax.experimental.pallas import tpu as pltpu

F32 = jnp.float32
BF16 = jnp.bfloat16

RMS_EPS = 1e-6
LRU_C = 8.0
HEAD_DIM = 128
ATTN_SCALE = HEAD_DIM ** -0.5
LOG2E = 1.4426950408889634
MASK_VALUE = -1e30

V7X_LANES = 128
V7X_SUBLANES = 8
V7X_VMEM_BYTES = 64 * 1024 * 1024
VMEM_LIMIT_BYTES = V7X_VMEM_BYTES - 5 * 1024 * 1024

RNN_BLOCK = 256
SCAN_GROUP = 8
CONV_W = 4
CONV_PAD_ROWS = V7X_SUBLANES

AUG_Q_LANES = (0, 1, 2)
AUG_K_LANES = (3, 4, 5)
ONES_ROWS = 16


def _params(*semantics):
    return pltpu.CompilerParams(dimension_semantics=semantics, vmem_limit_bytes=VMEM_LIMIT_BYTES)


def _rms(x, g):
    y = x * lax.rsqrt(jnp.mean(x * x, axis=-1, keepdims=True) + RMS_EPS)
    return y * g


def _ffn_kernel(x_ref, g_ref, wg_ref, wu_ref, wo_ref, fin_ref, *rest, apply_final, cast_next, row_chunk,
                col_chunk):
    if cast_next:
        src_in_ref, src_out_ref, o_ref, dst_in_ref, dst_out_ref, xn_ref = rest
        dst_in_ref[...] = src_in_ref[...].astype(BF16)
        dst_out_ref[...] = src_out_ref[...].astype(BF16)
    else:
        o_ref, xn_ref = rest
    j = pl.program_id(1)
    tm, d = x_ref.shape

    def by_rows(fn):
        def body(ri, carry):
            fn(pl.ds(pl.multiple_of(ri * row_chunk, row_chunk), row_chunk))
            return carry
        lax.fori_loop(0, tm // row_chunk, body, 0)

    @pl.when(j == 0)
    def _():
        def prep(rows):
            x = x_ref[rows, :]
            xn_ref[rows, :] = _rms(x, g_ref[...]).astype(BF16)
            o_ref[rows, :] = x
        by_rows(prep)

    xn = xn_ref[...]
    gate = jnp.dot(xn, wg_ref[...], preferred_element_type=F32)
    up = jnp.dot(xn, wu_ref[...], preferred_element_type=F32)
    h = (0.5 * (gate * jax.nn.sigmoid(gate)) * up).astype(BF16)
    for c in range(d // col_chunk):
        cols = slice(c * col_chunk, (c + 1) * col_chunk)
        o_ref[:, cols] += jnp.dot(h, wo_ref[:, cols], preferred_element_type=F32)

    if apply_final:
        @pl.when(j == pl.num_programs(1) - 1)
        def _():
            def norm(rows):
                o_ref[rows, :] = _rms(o_ref[rows, :], fin_ref[...])
            by_rows(norm)


def _ffn(x, g, w_in, w_out, fin, *, apply_final, tm, tf, cast_next=None):
    n, d = x.shape
    dff = w_out.shape[0]
    ni, nj = n // tm, dff // tf
    assert n % tm == 0 and dff % tf == 0 and w_in.shape == (d, 2 * dff)
    row_chunk, col_chunk = min(NORM_ROW_CHUNK, tm), min(FFN_COL_CHUNK, d)
    assert tm % row_chunk == 0 and d % col_chunk == 0
    in_specs = [
        pl.BlockSpec((tm, d), lambda i, j: (i, 0)),
        pl.BlockSpec((1, d), lambda i, j: (0, 0)),
        pl.BlockSpec((d, tf), lambda i, j: (0, j)),
        pl.BlockSpec((d, tf), lambda i, j: (0, j + nj)),
        pl.BlockSpec((tf, d), lambda i, j: (j, 0)),
        pl.BlockSpec((1, d), lambda i, j: (0, 0)),
    ]
    out_specs = [pl.BlockSpec((tm, d), lambda i, j: (i, 0))]
    out_shape = [jax.ShapeDtypeStruct((n, d), F32)]
    operands = [x, g.reshape(1, d), w_in, w_in, w_out, fin.reshape(1, d)]
    if cast_next is not None:
        src_in, src_out, layer, half = cast_next
        bi, bo = (d // ni, 2 * dff // nj), (dff // nj, d // ni)
        assert d % ni == 0 and bi[0] % (2 * V7X_SUBLANES) == 0 and bi[1] % V7X_LANES == 0
        assert bo[0] % (2 * V7X_SUBLANES) == 0 and bo[1] % V7X_LANES == 0
        in_specs += [pl.BlockSpec((None, None) + bi, lambda i, j: (layer, half, i, j)),
                     pl.BlockSpec((None, None) + bo, lambda i, j: (layer, half, j, i))]
        out_specs += [pl.BlockSpec(bi, lambda i, j: (i, j)), pl.BlockSpec(bo, lambda i, j: (j, i))]
        out_shape += [jax.ShapeDtypeStruct((d, 2 * dff), BF16), jax.ShapeDtypeStruct((dff, d), BF16)]
        operands += [src_in, src_out]
    outs = pl.pallas_call(
        functools.partial(_ffn_kernel, apply_final=apply_final, cast_next=cast_next is not None,
                          row_chunk=row_chunk, col_chunk=col_chunk),
        grid=(ni, nj),
        in_specs=in_specs,
        out_specs=out_specs,
        out_shape=out_shape,
        scratch_shapes=[pltpu.VMEM((tm, d), BF16)],
        compiler_params=_params("arbitrary", "arbitrary"),
        name="ffn",
    )(*operands)
    return outs if cast_next is not None else outs[0]


def _gelu_tanh(x):
    return 0.5 * x * (1.0 + jnp.tanh(0.7978845608028654 * (x + 0.044715 * (x * x * x))))


_EPILOGUES = {
    None: lambda y: y,
    "gelu": _gelu_tanh,
    "sigmoid": jax.nn.sigmoid,
    "attn_scale": lambda y: y * (ATTN_SCALE * LOG2E),
}


def _proj2_kernel(x_ref, g_ref, w_ref, oa_ref, ob_ref, *, a_epilogue, b_epilogue):
    half = w_ref.shape[1] // 2
    xn = _rms(x_ref[...], g_ref[...]).astype(BF16)
    a = jnp.dot(xn, w_ref[:, 0:half], preferred_element_type=F32)
    oa_ref[...] = _EPILOGUES[a_epilogue](a).astype(oa_ref.dtype)
    b = jnp.dot(xn, w_ref[:, half:], preferred_element_type=F32)
    ob_ref[...] = _EPILOGUES[b_epilogue](b).astype(ob_ref.dtype)


def _proj2(x, g, w, *, tm, a_epilogue=None, b_epilogue=None, a_dtype=F32):
    n, d = x.shape
    half = w.shape[1] // 2
    assert n % tm == 0
    return pl.pallas_call(
        functools.partial(_proj2_kernel, a_epilogue=a_epilogue, b_epilogue=b_epilogue),
        grid=(n // tm,),
        in_specs=[
            pl.BlockSpec((tm, d), lambda i: (i, 0)),
            pl.BlockSpec((1, d), lambda i: (0, 0)),
            pl.BlockSpec(w.shape, lambda i: (0, 0), pipeline_mode=pl.Buffered(1)),
        ],
        out_specs=[pl.BlockSpec((tm, half), lambda i: (i, 0))] * 2,
        out_shape=[jax.ShapeDtypeStruct((n, half), a_dtype), jax.ShapeDtypeStruct((n, half), F32)],
        compiler_params=_params("arbitrary"),
        name="proj2",
    )(x, g.reshape(1, d), w)


def _log_sigmoid(z):
    return jnp.minimum(z, 0.0) - jnp.log1p(jnp.exp(-jnp.abs(z)))


def _kv_kernel(x_ref, g_ref, w_ref, wf_ref, bf_ref, k_ref, v_ref, f_ref, *head_major_refs):
    half = w_ref.shape[1] // 2
    xn = _rms(x_ref[...], g_ref[...]).astype(BF16)
    k = jnp.dot(xn, w_ref[:, 0:half], preferred_element_type=F32)
    v = jnp.dot(xn, w_ref[:, half:], preferred_element_type=F32)
    k_ref[...] = k
    v_ref[...] = v
    z = jnp.dot(xn, wf_ref[...], preferred_element_type=F32) + bf_ref[...]
    f_ref[...] = _log_sigmoid(z)
    if head_major_refs:
        kh_ref, vh_ref = head_major_refs
        for head in range(half // HEAD_DIM):
            cols = slice(head * HEAD_DIM, (head + 1) * HEAD_DIM)
            kh_ref[head] = k[:, cols].astype(BF16)
            vh_ref[head] = v[:, cols].astype(BF16)


def _kv_proj(x, g, w_kv, wf_pad, bf_pad, *, tm, head_major):
    n, d = x.shape
    half = w_kv.shape[1] // 2
    nh = half // HEAD_DIM
    assert n % tm == 0
    out = jax.ShapeDtypeStruct((n, half), F32)
    out_specs = [
        pl.BlockSpec((tm, half), lambda i: (i, 0)),
        pl.BlockSpec((tm, half), lambda i: (i, 0)),
        pl.BlockSpec((tm, V7X_LANES), lambda i: (i, 0)),
    ]
    out_shape = [out, out, jax.ShapeDtypeStruct((n, V7X_LANES), F32)]
    if head_major:
        out_specs += [pl.BlockSpec((nh, tm, HEAD_DIM), lambda i: (0, i, 0))] * 2
        out_shape += [jax.ShapeDtypeStruct((nh, n, HEAD_DIM), BF16)] * 2
    return pl.pallas_call(
        _kv_kernel,
        grid=(n // tm,),
        in_specs=[
            pl.BlockSpec((tm, d), lambda i: (i, 0)),
            pl.BlockSpec((1, d), lambda i: (0, 0)),
            pl.BlockSpec(w_kv.shape, lambda i: (0, 0), pipeline_mode=pl.Buffered(1)),
            pl.BlockSpec((d, V7X_LANES), lambda i: (0, 0)),
            pl.BlockSpec((1, V7X_LANES), lambda i: (0, 0)),
        ],
        out_specs=out_specs,
        out_shape=out_shape,
        compiler_params=_params("arbitrary"),
        name="kv_proj",
    )(x, g.reshape(1, d), w_kv, wf_pad, bf_pad)


def _out_proj_kernel(a_ref, w_ref, x_ref, o_ref):
    o_ref[...] = x_ref[...] + jnp.dot(a_ref[...], w_ref[...], preferred_element_type=F32)


def _out_proj(a, w, x, *, tm):
    n, k = a.shape
    d = w.shape[1]
    assert n % tm == 0
    return pl.pallas_call(
        _out_proj_kernel,
        grid=(n // tm,),
        in_specs=[
            pl.BlockSpec((tm, k), lambda i: (i, 0)),
            pl.BlockSpec((k, d), lambda i: (0, 0)),
            pl.BlockSpec((tm, d), lambda i: (i, 0)),
        ],
        out_specs=pl.BlockSpec((tm, d), lambda i: (i, 0)),
        out_shape=jax.ShapeDtypeStruct((n, d), F32),
        compiler_params=_params("arbitrary"),
        name="out_proj",
    )(a, w, x)


def _softplus(y):
    return jnp.maximum(y, 0.0) + jnp.log1p(jnp.exp(-jnp.abs(y)))


def _scan_rows(a_ref, b_ref, hin_ref, h_in):
    g = SCAN_GROUP
    groups = a_ref.shape[0] // g

    def phase(k):
        return pl.ds(k, groups, stride=g)

    a_acc, b_acc = a_ref[phase(0), :], b_ref[phase(0), :]
    for k in range(1, g):
        ak = a_ref[phase(k), :]
        b_acc = ak * b_acc + b_ref[phase(k), :]
        a_acc = ak * a_acc
        a_ref[phase(k), :] = a_acc
        b_ref[phase(k), :] = b_acc
    hc = h_in
    for gi in range(groups):
        hin_ref[gi:gi + 1, :] = hc
        hc = a_acc[gi:gi + 1, :] * hc + b_acc[gi:gi + 1, :]
    hin = hin_ref[...]
    for k in range(g):
        b_ref[phase(k), :] = a_ref[phase(k), :] * hin + b_ref[phase(k), :]
    return hc


def _rglru_kernel(gate_ref, u_ref, conv0_ref, h0_ref, cw_ref, cb_ref, gw_ref, gb_ref, lam_ref,
                  hg_ref, nconv_ref, nh_ref, ubuf, a_s, b_s, hin_s, hcar, *, tt):
    c = u_ref.shape[-1]
    pad = CONV_PAD_ROWS

    @pl.when(pl.program_id(1) == 0)
    def _():
        ubuf[0:pad, :] = conv0_ref[0]
        hcar[...] = h0_ref[0]

    ubuf[pad:pad + tt, :] = u_ref[0]

    for blk in range(c // RNN_BLOCK):
        cols = slice(blk * RNN_BLOCK, (blk + 1) * RNN_BLOCK)
        u_blk = ubuf[pad:pad + tt, cols]
        tail_blk = ubuf[0:pad, cols]
        head_row = lax.broadcasted_iota(jnp.int32, (pad, RNN_BLOCK), 0)
        uc = cb_ref[:, cols]
        for k in range(CONV_W):
            back = CONV_W - 1 - k
            if back == 0:
                tap = u_blk
            else:
                rolled = pltpu.roll(u_blk, back, 0)
                head = jnp.where(head_row < back, pltpu.roll(tail_blk, back, 0), rolled[0:pad])
                tap = jnp.concatenate([head, rolled[pad:]], axis=0)
            uc = uc + tap * cw_ref[k:k + 1, cols]
        ub = uc.astype(BF16)
        r = jax.nn.sigmoid(jnp.dot(ub, gw_ref[0, blk], preferred_element_type=F32) + gb_ref[0:1, cols])
        i = jax.nn.sigmoid(jnp.dot(ub, gw_ref[1, blk], preferred_element_type=F32) + gb_ref[1:2, cols])
        log_a = (-LRU_C * r) * _softplus(-lam_ref[:, cols])
        a = jnp.exp(log_a)
        inp = jnp.sqrt(1.0 - a * a) * (i * uc)
        for j in range(RNN_BLOCK // V7X_LANES):
            slab = blk * (RNN_BLOCK // V7X_LANES) + j
            lanes = slice(slab * V7X_LANES, (slab + 1) * V7X_LANES)
            a_s[slab] = a[:, j * V7X_LANES:(j + 1) * V7X_LANES]
            b_s[slab] = inp[:, j * V7X_LANES:(j + 1) * V7X_LANES]
            hcar[:, lanes] = _scan_rows(a_s.at[slab], b_s.at[slab], hin_s.at[slab], hcar[:, lanes])
            hg_ref[0, :, lanes] = (b_s[slab] * gate_ref[0, :, lanes]).astype(BF16)

    tail = ubuf[pad + tt - (CONV_W - 1):pad + tt, :]
    nconv_ref[0] = tail
    ubuf[pad - (CONV_W - 1):pad, :] = tail
    nh_ref[0] = hcar[...]


def _rglru(gate, u, conv0, h0, cw, cb, gw, gb, lam, *, tt):
    b, t, c = u.shape
    assert t % tt == 0 and tt % V7X_SUBLANES == 0 and tt >= CONV_W - 1
    nblk = c // RNN_BLOCK
    conv0_pad = jnp.pad(conv0, ((0, 0), (CONV_PAD_ROWS - (CONV_W - 1), 0), (0, 0)))
    return pl.pallas_call(
        functools.partial(_rglru_kernel, tt=tt),
        grid=(b, t // tt),
        in_specs=[
            pl.BlockSpec((1, tt, c), lambda bi, ti: (bi, ti, 0)),
            pl.BlockSpec((1, tt, c), lambda bi, ti: (bi, ti, 0)),
            pl.BlockSpec((1, CONV_PAD_ROWS, c), lambda bi, ti: (bi, 0, 0)),
            pl.BlockSpec((1, 1, c), lambda bi, ti: (bi, 0, 0)),
            pl.BlockSpec((CONV_W, c), lambda bi, ti: (0, 0)),
            pl.BlockSpec((1, c), lambda bi, ti: (0, 0)),
            pl.BlockSpec((2, nblk, RNN_BLOCK, RNN_BLOCK), lambda bi, ti: (0, 0, 0, 0)),
            pl.BlockSpec((2, c), lambda bi, ti: (0, 0)),
            pl.BlockSpec((1, c), lambda bi, ti: (0, 0)),
        ],
        out_specs=[
            pl.BlockSpec((1, tt, c), lambda bi, ti: (bi, ti, 0)),
            pl.BlockSpec((1, CONV_W - 1, c), lambda bi, ti: (bi, 0, 0)),
            pl.BlockSpec((1, 1, c), lambda bi, ti: (bi, 0, 0)),
        ],
        out_shape=[
            jax.ShapeDtypeStruct((b, t, c), BF16),
            jax.ShapeDtypeStruct((b, CONV_W - 1, c), F32),
            jax.ShapeDtypeStruct((b, 1, c), F32),
        ],
        scratch_shapes=[
            pltpu.VMEM((CONV_PAD_ROWS + tt, c), F32),
            pltpu.VMEM((c // V7X_LANES, tt, V7X_LANES), F32),
            pltpu.VMEM((c // V7X_LANES, tt, V7X_LANES), F32),
            pltpu.VMEM((c // V7X_LANES, tt // SCAN_GROUP, V7X_LANES), F32),
            pltpu.VMEM((1, c), F32),
        ],
        compiler_params=_params("arbitrary", "arbitrary"),
        name="rglru",
    )(gate, u, conv0_pad, h0.reshape(b, 1, c), cw, cb.reshape(1, c), gw, gb, lam.reshape(1, c))


def _cumsum_kernel(x_ref, o_ref, *, rows):
    s = x_ref.shape[1]
    r = lax.broadcasted_iota(jnp.int32, (rows, rows), 0)
    q = lax.broadcasted_iota(jnp.int32, (rows, rows), 1)
    tril = (q <= r).astype(F32)

    def block(bi, carry):
        r0 = pl.multiple_of(bi * rows, rows)
        cs = jnp.dot(tril, x_ref[0, pl.ds(r0, rows), :], preferred_element_type=F32,
                     precision=lax.Precision.HIGHEST) + carry
        o_ref[0, pl.ds(r0, rows), :] = cs
        return cs[rows - 1:rows, :]

    lax.fori_loop(0, s // rows, block, jnp.zeros((1, x_ref.shape[2]), F32))


def _cumsum(x, *, rows):
    b, s, w = x.shape
    assert s % rows == 0
    return pl.pallas_call(
        functools.partial(_cumsum_kernel, rows=rows),
        grid=(b,),
        in_specs=[pl.BlockSpec((1, s, w), lambda bi: (bi, 0, 0))],
        out_specs=pl.BlockSpec((1, s, w), lambda bi: (bi, 0, 0)),
        out_shape=jax.ShapeDtypeStruct((b, s, w), F32),
        compiler_params=_params("arbitrary"),
        name="cumsum",
    )(x)


def _split3(c):
    hi = c.astype(BF16)
    r1 = c - hi.astype(F32)
    mid = r1.astype(BF16)
    lo = (r1 - mid.astype(F32)).astype(BF16)
    return hi, mid, lo


def _aug_pieces(c, head):
    w = c.shape[1]
    src = lax.broadcasted_iota(jnp.int32, (w, V7X_LANES), 0)
    dst = lax.broadcasted_iota(jnp.int32, (w, V7X_LANES), 1)
    out = None
    for piece, q_lane, k_lane in zip(_split3(c * LOG2E), AUG_Q_LANES, AUG_K_LANES):
        sign = jnp.where(dst == q_lane, 1.0, jnp.where(dst == k_lane, -1.0, 0.0))
        route = jnp.where(src == head, sign, 0.0).astype(BF16)
        term = jnp.dot(piece, route, preferred_element_type=F32)
        out = term if out is None else out + term
    return out


def _aug_side(pieces, one_lanes):
    lane_id = lax.broadcasted_iota(jnp.int32, pieces.shape, 1)
    ones = (lane_id >= one_lanes[0]) & (lane_id <= one_lanes[-1])
    return jnp.where(ones, 1.0, pieces).astype(BF16)


def _attn_prompt_kernel(q_ref, k_ref, v_ref, g_ref, c_ref, o_ref,
                        qa_ref, ka_ref, qx_ref, vt_ref, st_ref, mt_ref, m_ref, acc_ref, *, tile):
    head = pl.program_id(1)
    qi = pl.program_id(2)
    n_kt = vt_ref.shape[0]
    dims = (((1,), (1,)), ((), ()))

    @pl.when(qi == 0)
    def _():
        def build(bi, carry):
            rows = pl.ds(pl.multiple_of(bi * tile, tile), tile)
            ka_ref[rows, 0:HEAD_DIM] = k_ref[rows, :]
            pieces = _aug_pieces(c_ref[0, rows, :], head)
            ka_ref[rows, HEAD_DIM:] = _aug_side(pieces, AUG_Q_LANES)
            qx_ref[rows, :] = _aug_side(pieces, AUG_K_LANES)
            vt_ref[bi, 0:HEAD_DIM, :] = v_ref[rows, :].astype(F32).T.astype(BF16)
            vt_ref[bi, HEAD_DIM:, :] = jnp.ones((ONES_ROWS, tile), BF16)
            return carry
        lax.fori_loop(0, n_kt, build, 0)

    q0 = pl.multiple_of(qi * (2 * tile), 2 * tile)
    qa_ref[:, 0:HEAD_DIM] = q_ref[0]
    qa_ref[:, HEAD_DIM:] = qx_ref[pl.ds(q0, 2 * tile), :]
    m_ref[...] = jnp.full(m_ref.shape, MASK_VALUE, F32)
    acc_ref[...] = jnp.zeros(acc_ref.shape, F32)

    def scores(half, kj, slot):
        k0 = pl.multiple_of(kj * tile, tile)
        st = lax.dot_general(ka_ref[pl.ds(k0, tile), :], qa_ref[half * tile:(half + 1) * tile, :], dims,
                             preferred_element_type=F32)
        st_ref[slot] = st
        mt_ref[slot] = jnp.max(st, axis=0, keepdims=True)

    def update(half, kj, slot, masked):
        st = st_ref[slot]
        if masked:
            key = lax.broadcasted_iota(jnp.int32, (tile, tile), 0)
            qry = lax.broadcasted_iota(jnp.int32, (tile, tile), 1)
            st = jnp.where(key <= qry, st, MASK_VALUE)
            m_tile = jnp.max(st, axis=0, keepdims=True)
        else:
            m_tile = mt_ref[slot]
        m_prev = m_ref[half]
        m_new = jnp.maximum(m_prev, m_tile)
        alpha = jnp.exp2(m_prev - m_new)
        p = jnp.exp2(st - m_new).astype(BF16)
        acc_ref[half] = alpha * acc_ref[half] + jnp.dot(vt_ref[kj], p, preferred_element_type=F32)
        m_ref[half] = m_new

    scores(0, 0, 0)
    scores(1, 0, 2)

    def two_key_tiles(i, carry):
        kj = 2 * i
        scores(0, kj + 1, 1)
        scores(1, kj + 1, 3)
        update(0, kj, 0, False)
        update(1, kj, 2, False)
        scores(0, kj + 2, 0)
        scores(1, kj + 2, 2)
        update(0, kj + 1, 1, False)
        update(1, kj + 1, 3, False)
        return carry

    lax.fori_loop(0, qi, two_key_tiles, 0)
    scores(1, 2 * qi + 1, 3)
    update(0, 2 * qi, 0, True)
    update(1, 2 * qi, 2, False)
    update(1, 2 * qi + 1, 3, True)
    for half in range(2):
        rows = slice(half * tile, (half + 1) * tile)
        o = (acc_ref[half, 0:HEAD_DIM, :] / acc_ref[half, HEAD_DIM:HEAD_DIM + 1, :]).T
        o_ref[0, rows, :] = (o * g_ref[0, rows, :]).astype(BF16)


def _attn_prompt(q, k, v, g, c, *, tile):
    b, s, d = q.shape
    nh = d // HEAD_DIM
    assert s % (2 * tile) == 0 and tile % V7X_LANES == 0 and k.shape == (nh, b * s, HEAD_DIM)
    qspec = pl.BlockSpec((1, 2 * tile, HEAD_DIM), lambda bi, hi, qi: (bi, qi, hi))
    kspec = pl.BlockSpec((None, s, HEAD_DIM), lambda bi, hi, qi: (hi, bi, 0))
    return pl.pallas_call(
        functools.partial(_attn_prompt_kernel, tile=tile),
        grid=(b, nh, s // (2 * tile)),
        in_specs=[qspec, kspec, kspec, qspec,
                  pl.BlockSpec((1, s, V7X_LANES), lambda bi, hi, qi: (bi, 0, 0))],
        out_specs=qspec,
        out_shape=jax.ShapeDtypeStruct((b, s, d), BF16),
        scratch_shapes=[
            pltpu.VMEM((2 * tile, 2 * HEAD_DIM), BF16),
            pltpu.VMEM((s, 2 * HEAD_DIM), BF16),
            pltpu.VMEM((s, HEAD_DIM), BF16),
            pltpu.VMEM((s // tile, HEAD_DIM + ONES_ROWS, tile), BF16),
            pltpu.VMEM((4, tile, tile), F32),
            pltpu.VMEM((4, 1, tile), F32),
            pltpu.VMEM((2, 1, tile), F32),
            pltpu.VMEM((2, HEAD_DIM + ONES_ROWS, tile), F32),
        ],
        compiler_params=_params("arbitrary", "arbitrary", "arbitrary"),
        name="attn_prompt",
    )(q, k, v, g, c)


def _attn_sample_kernel(q_ref, kc_ref, vc_ref, kn_ref, vn_ref, g_ref, c_ref, o_ref):
    t = q_ref.shape[1]
    nh = q_ref.shape[2] // HEAD_DIM
    past = kc_ref.shape[1] // nh
    dims = (((1,), (1,)), ((), ()))
    row = lax.broadcasted_iota(jnp.int32, (t, t), 0)
    col = lax.broadcasted_iota(jnp.int32, (t, t), 1)
    c_all = c_ref[0]
    c_past, c_new = c_all[0:past], c_all[past:past + t]
    def scores(head):
        cols = slice(head * HEAD_DIM, (head + 1) * HEAD_DIM)
        new_pieces = _aug_pieces(c_new, head)
        qa = jnp.concatenate([q_ref[0, :, cols], _aug_side(new_pieces, AUG_K_LANES)], axis=1)
        kca = jnp.concatenate([kc_ref[0, pl.ds(head, past, stride=nh), :].astype(BF16),
                               _aug_side(_aug_pieces(c_past, head), AUG_Q_LANES)], axis=1)
        kna = jnp.concatenate([kn_ref[0, :, cols].astype(BF16), _aug_side(new_pieces, AUG_Q_LANES)], axis=1)
        s_past = lax.dot_general(qa, kca, dims, preferred_element_type=F32)
        s_new = jnp.where(col <= row, lax.dot_general(qa, kna, dims, preferred_element_type=F32), MASK_VALUE)
        return s_past, s_new

    def finish(head, s_past, s_new):
        cols = slice(head * HEAD_DIM, (head + 1) * HEAD_DIM)
        m = jnp.maximum(jnp.max(s_past, axis=1, keepdims=True), jnp.max(s_new, axis=1, keepdims=True))
        p_past = jnp.exp2(s_past - m)
        p_new = jnp.exp2(s_new - m)
        l = jnp.sum(p_past, axis=1, keepdims=True) + jnp.sum(p_new, axis=1, keepdims=True)
        v_past = vc_ref[0, pl.ds(head, past, stride=nh), :].astype(BF16)
        o = (jnp.dot(p_past.astype(BF16), v_past, preferred_element_type=F32)
             + jnp.dot(p_new.astype(BF16), vn_ref[0, :, cols].astype(BF16), preferred_element_type=F32)) / l
        o_ref[0, :, cols] = (o * g_ref[0, :, cols]).astype(BF16)

    pending = scores(0)
    for head in range(nh):
        current, pending = pending, (scores(head + 1) if head + 1 < nh else None)
        finish(head, *current)


def _attn_sample(q, kc, vc, kn, vn, g, c):
    b, t, d = q.shape
    _, past, nh, hd = kc.shape
    assert nh % V7X_SUBLANES == 0
    kc, vc = kc.reshape(b, past * nh, hd), vc.reshape(b, past * nh, hd)
    new = pl.BlockSpec((1, t, d), lambda bi: (bi, 0, 0))
    old = pl.BlockSpec((1, past * nh, hd), lambda bi: (bi, 0, 0))
    return pl.pallas_call(
        _attn_sample_kernel,
        grid=(b,),
        in_specs=[new, old, old, new, new, new,
                  pl.BlockSpec((1, past + t, V7X_LANES), lambda bi: (bi, 0, 0))],
        out_specs=new,
        out_shape=jax.ShapeDtypeStruct((b, t, d), BF16),
        compiler_params=_params("arbitrary"),
        name="attn_sample",
    )(q, kc, vc, kn, vn, g, c)


FFN_TM, FFN_TF = 1024, 512
NORM_ROW_CHUNK, FFN_COL_CHUNK = 256, 512
PROJ_TM = 256
OUT_TM = 512
RGLRU_TT_PROMPT = 256
CUMSUM_ROWS_PROMPT, CUMSUM_ROWS_SAMPLE = 256, 272
ATTN_TILE = 512


def kernel(x_prompt, x_sample, state_conv, state_h, cache_k, cache_v, cache_logf, ffn_norm, ffn_w_in,
           ffn_w_out, a_norm, a_w_in, a_conv_w, a_conv_b, a_gate_w, a_gate_b, a_lambda, a_w_out, kv_norm,
           w_kv, w_f, b_f, b_norm, b_w_qg, b_w_o, final_norm):
    bp, sp, d = x_prompt.shape
    bs, ts, _ = x_sample.shape
    depth = ffn_w_in.shape[0]
    n_a = a_w_in.shape[0]
    n_heads = w_f.shape[1]

    ffn_order = [(l, half) for l in range(depth) for half in range(2)]
    ffn_w = {ffn_order[0]: (ffn_w_in[0, 0].astype(BF16), ffn_w_out[0, 0].astype(BF16))}
    a_w_in_b, a_gate_w_b, a_w_out_b = a_w_in.astype(BF16), a_gate_w.astype(BF16), a_w_out.astype(BF16)
    w_kv_b, b_w_qg_b, b_w_o_b = w_kv.astype(BF16), b_w_qg.astype(BF16), b_w_o.astype(BF16)
    wf_pad = jnp.pad(w_f, ((0, 0), (0, V7X_LANES - n_heads))).astype(BF16)
    bf_pad = jnp.pad(b_f, (0, V7X_LANES - n_heads)).reshape(1, V7X_LANES)

    streams = {"p": x_prompt.reshape(bp * sp, d), "s": x_sample.reshape(bs * ts, d)}
    batch = {"p": (bp, sp), "s": (bs, ts)}
    conv_out = {"p": [], "s": []}
    h_out = {"p": [], "s": []}
    kv = {}

    def ffn(x, l, half, name, apply_final=False):
        w_in_b, w_out_b = ffn_w[(l, half)]
        pos = ffn_order.index((l, half))
        cast_next = None
        if name == "p" and pos + 1 < len(ffn_order):
            cast_next = (ffn_w_in, ffn_w_out) + ffn_order[pos + 1]
        out = _ffn(x, ffn_norm[l, half], w_in_b, w_out_b, final_norm, apply_final=apply_final, tm=FFN_TM,
                   tf=FFN_TF, cast_next=cast_next)
        if cast_next is None:
            return out
        ffn_w[ffn_order[pos + 1]] = (out[1], out[2])
        return out[0]

    for l in range(depth):
        for name in ("p", "s"):
            x = ffn(streams[name], l, 0, name)
            b, t = batch[name]
            if l < n_a:
                gate, u = _proj2(x, a_norm[l], a_w_in_b[l], tm=PROJ_TM, a_epilogue="gelu")
                c_rnn = u.shape[1]
                if name == "p":
                    conv0 = jnp.zeros((b, CONV_W - 1, c_rnn), F32)
                    h0 = jnp.zeros((b, c_rnn), F32)
                    tt = RGLRU_TT_PROMPT
                else:
                    conv0, h0, tt = state_conv[l], state_h[l], t
                hg, nconv, nh = _rglru(gate.reshape(b, t, c_rnn), u.reshape(b, t, c_rnn), conv0, h0,
                                       a_conv_w[l], a_conv_b[l], a_gate_w_b[l], a_gate_b[l], a_lambda[l], tt=tt)
                conv_out[name].append(nconv)
                h_out[name].append(nh.reshape(b, c_rnn))
                x = _out_proj(hg.reshape(b * t, c_rnn), a_w_out_b[l], x, tm=OUT_TM)
            else:
                jb = l - n_a
                q, g = _proj2(x, b_norm[jb], b_w_qg_b[jb], tm=PROJ_TM, a_epilogue="attn_scale",
                              b_epilogue="sigmoid", a_dtype=BF16)
                k, v, f_pad = kv[name][:3]
                if name == "p":
                    k_heads, v_heads = kv[name][3:]
                    c = _cumsum(f_pad.reshape(b, t, V7X_LANES), rows=CUMSUM_ROWS_PROMPT)
                    og = _attn_prompt(q.reshape(b, t, d), k_heads, v_heads, g.reshape(b, t, d), c,
                                      tile=ATTN_TILE)
                else:
                    cache_f_pad = jnp.pad(cache_logf, ((0, 0), (0, 0), (0, V7X_LANES - n_heads)))
                    f_all = jnp.concatenate([cache_f_pad, f_pad.reshape(b, t, V7X_LANES)], axis=1)
                    c = _cumsum(f_all, rows=CUMSUM_ROWS_SAMPLE)
                    og = _attn_sample(q.reshape(b, t, d), cache_k, cache_v, k.reshape(b, t, d),
                                      v.reshape(b, t, d), g.reshape(b, t, d), c)
                x = _out_proj(og.reshape(b * t, d), b_w_o_b[jb], x, tm=OUT_TM)
            x = ffn(x, l, 1, name, apply_final=(l == depth - 1))
            if l == n_a - 1:
                kv[name] = _kv_proj(x, kv_norm, w_kv_b, wf_pad, bf_pad, tm=PROJ_TM, head_major=(name == "p"))
            streams[name] = x

    def finish(name):
        b, t = batch[name]
        k, v, f_pad = kv[name][:3]
        return (streams[name].reshape(b, t, d), jnp.stack(conv_out[name]), jnp.stack(h_out[name]),
                k.reshape(b, t, n_heads, HEAD_DIM), v.reshape(b, t, n_heads, HEAD_DIM),
                f_pad[:, :n_heads].reshape(b, t, n_heads))

    yp, convp, hp, kp, vp, fp = finish("p")
    ys, convs, hs, ks, vs, fs = finish("s")
    return (yp, ys, convp, hp, kp, vp, fp, convs, hs, ks, vs, fs)
```

```python
import functools

import jax
import jax.numpy as jnp
from jax import lax
from jax.experimental import pallas as pl
from jax.experimental.pallas import tpu as pltpu

F32 = jnp.float32
BF16 = jnp.bfloat16

RMS_EPS = 1e-6
LRU_C = 8.0
HEAD_DIM = 128
ATTN_SCALE = HEAD_DIM ** -0.5
LOG2E = 1.4426950408889634
MASK_VALUE = -1e30

V7X_LANES = 128
V7X_SUBLANES = 8
V7X_VMEM_BYTES = 64 * 1024 * 1024
VMEM_LIMIT_BYTES = V7X_VMEM_BYTES - 5 * 1024 * 1024

RNN_BLOCK = 256
SCAN_GROUP = 8
CONV_W = 4
CONV_PAD_ROWS = V7X_SUBLANES

AUG_Q_LANES = (0, 1, 2)
AUG_K_LANES = (3, 4, 5)
ONES_ROWS = 16


def _params(*semantics):
    return pltpu.CompilerParams(dimension_semantics=semantics, vmem_limit_bytes=VMEM_LIMIT_BYTES)


def _rms(x, g):
    y = x * lax.rsqrt(jnp.mean(x * x, axis=-1, keepdims=True) + RMS_EPS)
    return y * g


def _ffn_kernel(x_ref, g_ref, wg_ref, wu_ref, wo_ref, fin_ref, *rest, apply_final, cast_next, row_chunk,
                col_chunk):
    if cast_next:
        src_in_ref, src_out_ref, o_ref, dst_in_ref, dst_out_ref, xn_ref = rest
        dst_in_ref[...] = src_in_ref[...].astype(BF16)
        dst_out_ref[...] = src_out_ref[...].astype(BF16)
    else:
        o_ref, xn_ref = rest
    j = pl.program_id(1)
    tm, d = x_ref.shape

    def by_rows(fn):
        def body(ri, carry):
            fn(pl.ds(pl.multiple_of(ri * row_chunk, row_chunk), row_chunk))
            return carry
        lax.fori_loop(0, tm // row_chunk, body, 0)

    @pl.when(j == 0)
    def _():
        def prep(rows):
            x = x_ref[rows, :]
            xn_ref[rows, :] = _rms(x, g_ref[...]).astype(BF16)
            o_ref[rows, :] = x
        by_rows(prep)

    xn = xn_ref[...]
    gate = jnp.dot(xn, wg_ref[...], preferred_element_type=F32)
    up = jnp.dot(xn, wu_ref[...], preferred_element_type=F32)
    h = (0.5 * (gate * jax.nn.sigmoid(gate)) * up).astype(BF16)
    for c in range(d // col_chunk):
        cols = slice(c * col_chunk, (c + 1) * col_chunk)
        o_ref[:, cols] += jnp.dot(h, wo_ref[:, cols], preferred_element_type=F32)

    if apply_final:
        @pl.when(j == pl.num_programs(1) - 1)
        def _():
            def norm(rows):
                o_ref[rows, :] = _rms(o_ref[rows, :], fin_ref[...])
            by_rows(norm)


def _ffn(x, g, w_in, w_out, fin, *, apply_final, tm, tf, cast_next=None):
    n, d = x.shape
    dff = w_out.shape[0]
    ni, nj = n // tm, dff // tf
    assert n % tm == 0 and dff % tf == 0 and w_in.shape == (d, 2 * dff)
    row_chunk, col_chunk = min(NORM_ROW_CHUNK, tm), min(FFN_COL_CHUNK, d)
    assert tm % row_chunk == 0 and d % col_chunk == 0
    in_specs = [
        pl.BlockSpec((tm, d), lambda i, j: (i, 0)),
        pl.BlockSpec((1, d), lambda i, j: (0, 0)),
        pl.BlockSpec((d, tf), lambda i, j: (0, j)),
        pl.BlockSpec((d, tf), lambda i, j: (0, j + nj)),
        pl.BlockSpec((tf, d), lambda i, j: (j, 0)),
        pl.BlockSpec((1, d), lambda i, j: (0, 0)),
    ]
    out_specs = [pl.BlockSpec((tm, d), lambda i, j: (i, 0))]
    out_shape = [jax.ShapeDtypeStruct((n, d), F32)]
    operands = [x, g.reshape(1, d), w_in, w_in, w_out, fin.reshape(1, d)]
    if cast_next is not None:
        src_in, src_out, layer, half = cast_next
        bi, bo = (d // ni, 2 * dff // nj), (dff // nj, d // ni)
        assert d % ni == 0 and bi[0] % (2 * V7X_SUBLANES) == 0 and bi[1] % V7X_LANES == 0
        assert bo[0] % (2 * V7X_SUBLANES) == 0 and bo[1] % V7X_LANES == 0
        in_specs += [pl.BlockSpec((None, None) + bi, lambda i, j: (layer, half, i, j)),
                     pl.BlockSpec((None, None) + bo, lambda i, j: (layer, half, j, i))]
        out_specs += [pl.BlockSpec(bi, lambda i, j: (i, j)), pl.BlockSpec(bo, lambda i, j: (j, i))]
        out_shape += [jax.ShapeDtypeStruct((d, 2 * dff), BF16), jax.ShapeDtypeStruct((dff, d), BF16)]
        operands += [src_in, src_out]
    outs = pl.pallas_call(
        functools.partial(_ffn_kernel, apply_final=apply_final, cast_next=cast_next is not None,
                          row_chunk=row_chunk, col_chunk=col_chunk),
        grid=(ni, nj),
        in_specs=in_specs,
        out_specs=out_specs,
        out_shape=out_shape,
        scratch_shapes=[pltpu.VMEM((tm, d), BF16)],
        compiler_params=_params("arbitrary", "arbitrary"),
        name="ffn",
    )(*operands)
    return outs if cast_next is not None else outs[0]


def _gelu_tanh(x):
    return 0.5 * x * (1.0 + jnp.tanh(0.7978845608028654 * (x + 0.044715 * (x * x * x))))


_EPILOGUES = {
    None: lambda y: y,
    "gelu": _gelu_tanh,
    "sigmoid": jax.nn.sigmoid,
    "attn_scale": lambda y: y * (ATTN_SCALE * LOG2E),
}


def _proj2_kernel(x_ref, g_ref, w_ref, *rest, a_epilogue, b_epilogue, n_cast):
    src_refs, (oa_ref, ob_ref), dst_refs = rest[:n_cast], rest[n_cast:n_cast + 2], rest[n_cast + 2:]
    for src_ref, dst_ref in zip(src_refs, dst_refs):
        dst_ref[...] = src_ref[...].astype(BF16)
    half = w_ref.shape[1] // 2
    xn = _rms(x_ref[...], g_ref[...]).astype(BF16)
    a = jnp.dot(xn, w_ref[:, 0:half], preferred_element_type=F32)
    oa_ref[...] = _EPILOGUES[a_epilogue](a).astype(oa_ref.dtype)
    b = jnp.dot(xn, w_ref[:, half:], preferred_element_type=F32)
    ob_ref[...] = _EPILOGUES[b_epilogue](b).astype(ob_ref.dtype)


def _proj2(x, g, w, *, tm, a_epilogue=None, b_epilogue=None, a_dtype=F32, cast=()):
    n, d = x.shape
    half = w.shape[1] // 2
    steps = n // tm
    assert n % tm == 0
    cast_specs = []
    for src in cast:
        rows = src.shape[0] // steps
        assert src.shape[0] % steps == 0 and rows % (2 * V7X_SUBLANES) == 0 and src.shape[1] % V7X_LANES == 0
        cast_specs.append(pl.BlockSpec((rows, src.shape[1]), lambda i: (i, 0)))
    return pl.pallas_call(
        functools.partial(_proj2_kernel, a_epilogue=a_epilogue, b_epilogue=b_epilogue, n_cast=len(cast)),
        grid=(steps,),
        in_specs=[
            pl.BlockSpec((tm, d), lambda i: (i, 0)),
            pl.BlockSpec((1, d), lambda i: (0, 0)),
            pl.BlockSpec(w.shape, lambda i: (0, 0), pipeline_mode=pl.Buffered(1)),
        ] + cast_specs,
        out_specs=[pl.BlockSpec((tm, half), lambda i: (i, 0))] * 2 + cast_specs,
        out_shape=[jax.ShapeDtypeStruct((n, half), a_dtype), jax.ShapeDtypeStruct((n, half), F32)]
        + [jax.ShapeDtypeStruct(src.shape, BF16) for src in cast],
        compiler_params=_params("arbitrary"),
        name="proj2",
    )(x, g.reshape(1, d), w, *cast)


def _log_sigmoid(z):
    return jnp.minimum(z, 0.0) - jnp.log1p(jnp.exp(-jnp.abs(z)))


def _kv_kernel(x_ref, g_ref, w_ref, wf_ref, bf_ref, k_ref, v_ref, f_ref, *head_major_refs):
    half = w_ref.shape[1] // 2
    xn = _rms(x_ref[...], g_ref[...]).astype(BF16)
    k = jnp.dot(xn, w_ref[:, 0:half], preferred_element_type=F32)
    v = jnp.dot(xn, w_ref[:, half:], preferred_element_type=F32)
    k_ref[...] = k
    v_ref[...] = v
    z = jnp.dot(xn, wf_ref[...], preferred_element_type=F32) + bf_ref[...]
    f_ref[...] = _log_sigmoid(z)
    if head_major_refs:
        kh_ref, vh_ref = head_major_refs
        for head in range(half // HEAD_DIM):
            cols = slice(head * HEAD_DIM, (head + 1) * HEAD_DIM)
            kh_ref[head] = k[:, cols].astype(BF16)
            vh_ref[head] = v[:, cols].astype(BF16)


def _kv_proj(x, g, w_kv, wf_pad, bf_pad, *, tm, head_major):
    n, d = x.shape
    half = w_kv.shape[1] // 2
    nh = half // HEAD_DIM
    assert n % tm == 0
    out = jax.ShapeDtypeStruct((n, half), F32)
    out_specs = [
        pl.BlockSpec((tm, half), lambda i: (i, 0)),
        pl.BlockSpec((tm, half), lambda i: (i, 0)),
        pl.BlockSpec((tm, V7X_LANES), lambda i: (i, 0)),
    ]
    out_shape = [out, out, jax.ShapeDtypeStruct((n, V7X_LANES), F32)]
    if head_major:
        out_specs += [pl.BlockSpec((nh, tm, HEAD_DIM), lambda i: (0, i, 0))] * 2
        out_shape += [jax.ShapeDtypeStruct((nh, n, HEAD_DIM), BF16)] * 2
    return pl.pallas_call(
        _kv_kernel,
        grid=(n // tm,),
        in_specs=[
            pl.BlockSpec((tm, d), lambda i: (i, 0)),
            pl.BlockSpec((1, d), lambda i: (0, 0)),
            pl.BlockSpec(w_kv.shape, lambda i: (0, 0), pipeline_mode=pl.Buffered(1)),
            pl.BlockSpec((d, V7X_LANES), lambda i: (0, 0)),
            pl.BlockSpec((1, V7X_LANES), lambda i: (0, 0)),
        ],
        out_specs=out_specs,
        out_shape=out_shape,
        compiler_params=_params("arbitrary"),
        name="kv_proj",
    )(x, g.reshape(1, d), w_kv, wf_pad, bf_pad)


def _out_proj_kernel(a_ref, w_ref, x_ref, o_ref):
    o_ref[...] = x_ref[...] + jnp.dot(a_ref[...], w_ref[...], preferred_element_type=F32)


def _out_proj(a, w, x, *, tm):
    n, k = a.shape
    d = w.shape[1]
    assert n % tm == 0
    return pl.pallas_call(
        _out_proj_kernel,
        grid=(n // tm,),
        in_specs=[
            pl.BlockSpec((tm, k), lambda i: (i, 0)),
            pl.BlockSpec((k, d), lambda i: (0, 0)),
            pl.BlockSpec((tm, d), lambda i: (i, 0)),
        ],
        out_specs=pl.BlockSpec((tm, d), lambda i: (i, 0)),
        out_shape=jax.ShapeDtypeStruct((n, d), F32),
        compiler_params=_params("arbitrary"),
        name="out_proj",
    )(a, w, x)


def _softplus(y):
    return jnp.maximum(y, 0.0) + jnp.log1p(jnp.exp(-jnp.abs(y)))


def _scan_rows(a_ref, b_ref, hin_ref, h_in):
    g = SCAN_GROUP
    groups = a_ref.shape[0] // g

    def phase(k):
        return pl.ds(k, groups, stride=g)

    a_acc, b_acc = a_ref[phase(0), :], b_ref[phase(0), :]
    for k in range(1, g):
        ak = a_ref[phase(k), :]
        b_acc = ak * b_acc + b_ref[phase(k), :]
        a_acc = ak * a_acc
        a_ref[phase(k), :] = a_acc
        b_ref[phase(k), :] = b_acc
    hc = h_in
    for gi in range(groups):
        hin_ref[gi:gi + 1, :] = hc
        hc = a_acc[gi:gi + 1, :] * hc + b_acc[gi:gi + 1, :]
    hin = hin_ref[...]
    for k in range(g):
        b_ref[phase(k), :] = a_ref[phase(k), :] * hin + b_ref[phase(k), :]
    return hc


def _rglru_kernel(gate_ref, u_ref, conv0_ref, h0_ref, cw_ref, cb_ref, gw_ref, gb_ref, lam_ref,
                  hg_ref, nconv_ref, nh_ref, ubuf, a_s, b_s, hin_s, hcar, *, tt):
    c = u_ref.shape[-1]
    pad = CONV_PAD_ROWS

    @pl.when(pl.program_id(1) == 0)
    def _():
        ubuf[0:pad, :] = conv0_ref[0]
        hcar[...] = h0_ref[0]

    ubuf[pad:pad + tt, :] = u_ref[0]

    for blk in range(c // RNN_BLOCK):
        cols = slice(blk * RNN_BLOCK, (blk + 1) * RNN_BLOCK)
        u_blk = ubuf[pad:pad + tt, cols]
        tail_blk = ubuf[0:pad, cols]
        head_row = lax.broadcasted_iota(jnp.int32, (pad, RNN_BLOCK), 0)
        uc = cb_ref[:, cols]
        for k in range(CONV_W):
            back = CONV_W - 1 - k
            if back == 0:
                tap = u_blk
            else:
                rolled = pltpu.roll(u_blk, back, 0)
                head = jnp.where(head_row < back, pltpu.roll(tail_blk, back, 0), rolled[0:pad])
                tap = jnp.concatenate([head, rolled[pad:]], axis=0)
            uc = uc + tap * cw_ref[k:k + 1, cols]
        ub = uc.astype(BF16)
        r = jax.nn.sigmoid(jnp.dot(ub, gw_ref[0, blk], preferred_element_type=F32) + gb_ref[0:1, cols])
        i = jax.nn.sigmoid(jnp.dot(ub, gw_ref[1, blk], preferred_element_type=F32) + gb_ref[1:2, cols])
        log_a = (-LRU_C * r) * _softplus(-lam_ref[:, cols])
        a = jnp.exp(log_a)
        inp = jnp.sqrt(1.0 - a * a) * (i * uc)
        for j in range(RNN_BLOCK // V7X_LANES):
            slab = blk * (RNN_BLOCK // V7X_LANES) + j
            lanes = slice(slab * V7X_LANES, (slab + 1) * V7X_LANES)
            a_s[slab] = a[:, j * V7X_LANES:(j + 1) * V7X_LANES]
            b_s[slab] = inp[:, j * V7X_LANES:(j + 1) * V7X_LANES]
            hcar[:, lanes] = _scan_rows(a_s.at[slab], b_s.at[slab], hin_s.at[slab], hcar[:, lanes])
            hg_ref[0, :, lanes] = (b_s[slab] * gate_ref[0, :, lanes]).astype(BF16)

    tail = ubuf[pad + tt - (CONV_W - 1):pad + tt, :]
    nconv_ref[0] = tail
    ubuf[pad - (CONV_W - 1):pad, :] = tail
    nh_ref[0] = hcar[...]


def _rglru(gate, u, conv0, h0, cw, cb, gw, gb, lam, *, tt):
    b, t, c = u.shape
    assert t % tt == 0 and tt % V7X_SUBLANES == 0 and tt >= CONV_W - 1
    nblk = c // RNN_BLOCK
    conv0_pad = jnp.pad(conv0, ((0, 0), (CONV_PAD_ROWS - (CONV_W - 1), 0), (0, 0)))
    return pl.pallas_call(
        functools.partial(_rglru_kernel, tt=tt),
        grid=(b, t // tt),
        in_specs=[
            pl.BlockSpec((1, tt, c), lambda bi, ti: (bi, ti, 0)),
            pl.BlockSpec((1, tt, c), lambda bi, ti: (bi, ti, 0)),
            pl.BlockSpec((1, CONV_PAD_ROWS, c), lambda bi, ti: (bi, 0, 0)),
            pl.BlockSpec((1, 1, c), lambda bi, ti: (bi, 0, 0)),
            pl.BlockSpec((CONV_W, c), lambda bi, ti: (0, 0)),
            pl.BlockSpec((1, c), lambda bi, ti: (0, 0)),
            pl.BlockSpec((2, nblk, RNN_BLOCK, RNN_BLOCK), lambda bi, ti: (0, 0, 0, 0)),
            pl.BlockSpec((2, c), lambda bi, ti: (0, 0)),
            pl.BlockSpec((1, c), lambda bi, ti: (0, 0)),
        ],
        out_specs=[
            pl.BlockSpec((1, tt, c), lambda bi, ti: (bi, ti, 0)),
            pl.BlockSpec((1, CONV_W - 1, c), lambda bi, ti: (bi, 0, 0)),
            pl.BlockSpec((1, 1, c), lambda bi, ti: (bi, 0, 0)),
        ],
        out_shape=[
            jax.ShapeDtypeStruct((b, t, c), BF16),
            jax.ShapeDtypeStruct((b, CONV_W - 1, c), F32),
            jax.ShapeDtypeStruct((b, 1, c), F32),
        ],
        scratch_shapes=[
            pltpu.VMEM((CONV_PAD_ROWS + tt, c), F32),
            pltpu.VMEM((c // V7X_LANES, tt, V7X_LANES), F32),
            pltpu.VMEM((c // V7X_LANES, tt, V7X_LANES), F32),
            pltpu.VMEM((c // V7X_LANES, tt // SCAN_GROUP, V7X_LANES), F32),
            pltpu.VMEM((1, c), F32),
        ],
        compiler_params=_params("arbitrary", "arbitrary"),
        name="rglru",
    )(gate, u, conv0_pad, h0.reshape(b, 1, c), cw, cb.reshape(1, c), gw, gb, lam.reshape(1, c))


def _cumsum_kernel(x_ref, o_ref, *, rows):
    s = x_ref.shape[1]
    r = lax.broadcasted_iota(jnp.int32, (rows, rows), 0)
    q = lax.broadcasted_iota(jnp.int32, (rows, rows), 1)
    tril = (q <= r).astype(F32)

    def block(bi, carry):
        r0 = pl.multiple_of(bi * rows, rows)
        cs = jnp.dot(tril, x_ref[0, pl.ds(r0, rows), :], preferred_element_type=F32,
                     precision=lax.Precision.HIGHEST) + carry
        o_ref[0, pl.ds(r0, rows), :] = cs
        return cs[rows - 1:rows, :]

    lax.fori_loop(0, s // rows, block, jnp.zeros((1, x_ref.shape[2]), F32))


def _cumsum(x, *, rows):
    b, s, w = x.shape
    assert s % rows == 0
    return pl.pallas_call(
        functools.partial(_cumsum_kernel, rows=rows),
        grid=(b,),
        in_specs=[pl.BlockSpec((1, s, w), lambda bi: (bi, 0, 0))],
        out_specs=pl.BlockSpec((1, s, w), lambda bi: (bi, 0, 0)),
        out_shape=jax.ShapeDtypeStruct((b, s, w), F32),
        compiler_params=_params("arbitrary"),
        name="cumsum",
    )(x)


def _split3(c):
    hi = c.astype(BF16)
    r1 = c - hi.astype(F32)
    mid = r1.astype(BF16)
    lo = (r1 - mid.astype(F32)).astype(BF16)
    return hi, mid, lo


def _aug_pieces(c, head):
    w = c.shape[1]
    src = lax.broadcasted_iota(jnp.int32, (w, V7X_LANES), 0)
    dst = lax.broadcasted_iota(jnp.int32, (w, V7X_LANES), 1)
    out = None
    for piece, q_lane, k_lane in zip(_split3(c * LOG2E), AUG_Q_LANES, AUG_K_LANES):
        sign = jnp.where(dst == q_lane, 1.0, jnp.where(dst == k_lane, -1.0, 0.0))
        route = jnp.where(src == head, sign, 0.0).astype(BF16)
        term = jnp.dot(piece, route, preferred_element_type=F32)
        out = term if out is None else out + term
    return out


def _aug_side(pieces, one_lanes):
    lane_id = lax.broadcasted_iota(jnp.int32, pieces.shape, 1)
    ones = (lane_id >= one_lanes[0]) & (lane_id <= one_lanes[-1])
    return jnp.where(ones, 1.0, pieces).astype(BF16)


def _attn_prompt_kernel(q_ref, k_ref, v_ref, g_ref, c_ref, o_ref,
                        qa_ref, ka_ref, qx_ref, vt_ref, st_ref, mt_ref, m_ref, acc_ref, *, tile):
    head = pl.program_id(1)
    qi = pl.program_id(2)
    n_kt = vt_ref.shape[0]
    dims = (((1,), (1,)), ((), ()))

    @pl.when(qi == 0)
    def _():
        def build(bi, carry):
            rows = pl.ds(pl.multiple_of(bi * tile, tile), tile)
            ka_ref[rows, 0:HEAD_DIM] = k_ref[rows, :]
            pieces = _aug_pieces(c_ref[0, rows, :], head)
            ka_ref[rows, HEAD_DIM:] = _aug_side(pieces, AUG_Q_LANES)
            qx_ref[rows, :] = _aug_side(pieces, AUG_K_LANES)
            vt_ref[bi, 0:HEAD_DIM, :] = v_ref[rows, :].astype(F32).T.astype(BF16)
            vt_ref[bi, HEAD_DIM:, :] = jnp.ones((ONES_ROWS, tile), BF16)
            return carry
        lax.fori_loop(0, n_kt, build, 0)

    q0 = pl.multiple_of(qi * (2 * tile), 2 * tile)
    qa_ref[:, 0:HEAD_DIM] = q_ref[0]
    qa_ref[:, HEAD_DIM:] = qx_ref[pl.ds(q0, 2 * tile), :]
    m_ref[...] = jnp.full(m_ref.shape, MASK_VALUE, F32)
    acc_ref[...] = jnp.zeros(acc_ref.shape, F32)

    def scores(half, kj, slot):
        k0 = pl.multiple_of(kj * tile, tile)
        st = lax.dot_general(ka_ref[pl.ds(k0, tile), :], qa_ref[half * tile:(half + 1) * tile, :], dims,
                             preferred_element_type=F32)
        st_ref[slot] = st
        mt_ref[slot] = jnp.max(st, axis=0, keepdims=True)

    def update(half, kj, slot, masked):
        st = st_ref[slot]
        if masked:
            key = lax.broadcasted_iota(jnp.int32, (tile, tile), 0)
            qry = lax.broadcasted_iota(jnp.int32, (tile, tile), 1)
            st = jnp.where(key <= qry, st, MASK_VALUE)
            m_tile = jnp.max(st, axis=0, keepdims=True)
        else:
            m_tile = mt_ref[slot]
        m_prev = m_ref[half]
        m_new = jnp.maximum(m_prev, m_tile)
        alpha = jnp.exp2(m_prev - m_new)
        p = jnp.exp2(st - m_new).astype(BF16)
        acc_ref[half] = alpha * acc_ref[half] + jnp.dot(vt_ref[kj], p, preferred_element_type=F32)
        m_ref[half] = m_new

    scores(0, 0, 0)
    scores(1, 0, 2)

    def two_key_tiles(i, carry):
        kj = 2 * i
        scores(0, kj + 1, 1)
        scores(1, kj + 1, 3)
        update(0, kj, 0, False)
        update(1, kj, 2, False)
        scores(0, kj + 2, 0)
        scores(1, kj + 2, 2)
        update(0, kj + 1, 1, False)
        update(1, kj + 1, 3, False)
        return carry

    lax.fori_loop(0, qi, two_key_tiles, 0)
    scores(1, 2 * qi + 1, 3)
    update(0, 2 * qi, 0, True)
    update(1, 2 * qi, 2, False)
    update(1, 2 * qi + 1, 3, True)
    for half in range(2):
        rows = slice(half * tile, (half + 1) * tile)
        o = (acc_ref[half, 0:HEAD_DIM, :] / acc_ref[half, HEAD_DIM:HEAD_DIM + 1, :]).T
        o_ref[0, rows, :] = (o * g_ref[0, rows, :]).astype(BF16)


def _attn_prompt(q, k, v, g, c, *, tile):
    b, s, d = q.shape
    nh = d // HEAD_DIM
    assert s % (2 * tile) == 0 and tile % V7X_LANES == 0 and k.shape == (nh, b * s, HEAD_DIM)
    qspec = pl.BlockSpec((1, 2 * tile, HEAD_DIM), lambda bi, hi, qi: (bi, qi, hi))
    kspec = pl.BlockSpec((None, s, HEAD_DIM), lambda bi, hi, qi: (hi, bi, 0))
    return pl.pallas_call(
        functools.partial(_attn_prompt_kernel, tile=tile),
        grid=(b, nh, s // (2 * tile)),
        in_specs=[qspec, kspec, kspec, qspec,
                  pl.BlockSpec((1, s, V7X_LANES), lambda bi, hi, qi: (bi, 0, 0))],
        out_specs=qspec,
        out_shape=jax.ShapeDtypeStruct((b, s, d), BF16),
        scratch_shapes=[
            pltpu.VMEM((2 * tile, 2 * HEAD_DIM), BF16),
            pltpu.VMEM((s, 2 * HEAD_DIM), BF16),
            pltpu.VMEM((s, HEAD_DIM), BF16),
            pltpu.VMEM((s // tile, HEAD_DIM + ONES_ROWS, tile), BF16),
            pltpu.VMEM((4, tile, tile), F32),
            pltpu.VMEM((4, 1, tile), F32),
            pltpu.VMEM((2, 1, tile), F32),
            pltpu.VMEM((2, HEAD_DIM + ONES_ROWS, tile), F32),
        ],
        compiler_params=_params("arbitrary", "arbitrary", "arbitrary"),
        name="attn_prompt",
    )(q, k, v, g, c)


def _attn_sample_kernel(q_ref, kc_ref, vc_ref, kn_ref, vn_ref, g_ref, c_ref, o_ref):
    t = q_ref.shape[1]
    nh = q_ref.shape[2] // HEAD_DIM
    past = kc_ref.shape[1] // nh
    dims = (((1,), (1,)), ((), ()))
    row = lax.broadcasted_iota(jnp.int32, (t, t), 0)
    col = lax.broadcasted_iota(jnp.int32, (t, t), 1)
    c_all = c_ref[0]
    c_past, c_new = c_all[0:past], c_all[past:past + t]
    def scores(head):
        cols = slice(head * HEAD_DIM, (head + 1) * HEAD_DIM)
        new_pieces = _aug_pieces(c_new, head)
        qa = jnp.concatenate([q_ref[0, :, cols], _aug_side(new_pieces, AUG_K_LANES)], axis=1)
        kca = jnp.concatenate([kc_ref[0, pl.ds(head, past, stride=nh), :].astype(BF16),
                               _aug_side(_aug_pieces(c_past, head), AUG_Q_LANES)], axis=1)
        kna = jnp.concatenate([kn_ref[0, :, cols].astype(BF16), _aug_side(new_pieces, AUG_Q_LANES)], axis=1)
        s_past = lax.dot_general(qa, kca, dims, preferred_element_type=F32)
        s_new = jnp.where(col <= row, lax.dot_general(qa, kna, dims, preferred_element_type=F32), MASK_VALUE)
        return s_past, s_new

    def finish(head, s_past, s_new):
        cols = slice(head * HEAD_DIM, (head + 1) * HEAD_DIM)
        m = jnp.maximum(jnp.max(s_past, axis=1, keepdims=True), jnp.max(s_new, axis=1, keepdims=True))
        p_past = jnp.exp2(s_past - m)
        p_new = jnp.exp2(s_new - m)
        l = jnp.sum(p_past, axis=1, keepdims=True) + jnp.sum(p_new, axis=1, keepdims=True)
        v_past = vc_ref[0, pl.ds(head, past, stride=nh), :].astype(BF16)
        o = (jnp.dot(p_past.astype(BF16), v_past, preferred_element_type=F32)
             + jnp.dot(p_new.astype(BF16), vn_ref[0, :, cols].astype(BF16), preferred_element_type=F32)) / l
        o_ref[0, :, cols] = (o * g_ref[0, :, cols]).astype(BF16)

    pending = scores(0)
    for head in range(nh):
        current, pending = pending, (scores(head + 1) if head + 1 < nh else None)
        finish(head, *current)


def _attn_sample(q, kc, vc, kn, vn, g, c):
    b, t, d = q.shape
    _, past, nh, hd = kc.shape
    assert nh % V7X_SUBLANES == 0
    kc, vc = kc.reshape(b, past * nh, hd), vc.reshape(b, past * nh, hd)
    new = pl.BlockSpec((1, t, d), lambda bi: (bi, 0, 0))
    old = pl.BlockSpec((1, past * nh, hd), lambda bi: (bi, 0, 0))
    return pl.pallas_call(
        _attn_sample_kernel,
        grid=(b,),
        in_specs=[new, old, old, new, new, new,
                  pl.BlockSpec((1, past + t, V7X_LANES), lambda bi: (bi, 0, 0))],
        out_specs=new,
        out_shape=jax.ShapeDtypeStruct((b, t, d), BF16),
        compiler_params=_params("arbitrary"),
        name="attn_sample",
    )(q, kc, vc, kn, vn, g, c)


FFN_TM, FFN_TF = 1024, 512
NORM_ROW_CHUNK, FFN_COL_CHUNK = 256, 512
PROJ_TM = 256
OUT_TM = 512
RGLRU_TT_PROMPT = 256
CUMSUM_ROWS_PROMPT, CUMSUM_ROWS_SAMPLE = 256, 272
ATTN_TILE = 512


def kernel(x_prompt, x_sample, state_conv, state_h, cache_k, cache_v, cache_logf, ffn_norm, ffn_w_in,
           ffn_w_out, a_norm, a_w_in, a_conv_w, a_conv_b, a_gate_w, a_gate_b, a_lambda, a_w_out, kv_norm,
           w_kv, w_f, b_f, b_norm, b_w_qg, b_w_o, final_norm):
    bp, sp, d = x_prompt.shape
    bs, ts, _ = x_sample.shape
    depth = ffn_w_in.shape[0]
    n_a = a_w_in.shape[0]
    n_heads = w_f.shape[1]

    ffn_order = [(l, half) for l in range(depth) for half in range(2)]
    ffn_w = {ffn_order[0]: (ffn_w_in[0, 0].astype(BF16), ffn_w_out[0, 0].astype(BF16))}
    a_w_in_b, a_gate_w_b, a_w_out_b = a_w_in.astype(BF16), a_gate_w.astype(BF16), a_w_out.astype(BF16)
    late_src = [w_kv] + [b_w_qg[j] for j in range(b_w_qg.shape[0])] + [b_w_o[j] for j in range(b_w_o.shape[0])]
    late_w = [] if n_a > 0 else [wl.astype(BF16) for wl in late_src]
    n_b = b_w_qg.shape[0]
    wf_pad = jnp.pad(w_f, ((0, 0), (0, V7X_LANES - n_heads))).astype(BF16)
    bf_pad = jnp.pad(b_f, (0, V7X_LANES - n_heads)).reshape(1, V7X_LANES)

    streams = {"p": x_prompt.reshape(bp * sp, d), "s": x_sample.reshape(bs * ts, d)}
    batch = {"p": (bp, sp), "s": (bs, ts)}
    conv_out = {"p": [], "s": []}
    h_out = {"p": [], "s": []}
    kv = {}

    def ffn(x, l, half, name, apply_final=False):
        w_in_b, w_out_b = ffn_w[(l, half)]
        pos = ffn_order.index((l, half))
        cast_next = None
        if name == "p" and pos + 1 < len(ffn_order):
            cast_next = (ffn_w_in, ffn_w_out) + ffn_order[pos + 1]
        out = _ffn(x, ffn_norm[l, half], w_in_b, w_out_b, final_norm, apply_final=apply_final, tm=FFN_TM,
                   tf=FFN_TF, cast_next=cast_next)
        if cast_next is None:
            return out
        ffn_w[ffn_order[pos + 1]] = (out[1], out[2])
        return out[0]

    for l in range(depth):
        for name in ("p", "s"):
            x = ffn(streams[name], l, 0, name)
            b, t = batch[name]
            if l < n_a:
                cast = late_src if (l == 0 and name == "p") else ()
                gate, u, *cast_out = _proj2(x, a_norm[l], a_w_in_b[l], tm=PROJ_TM, a_epilogue="gelu",
                                            cast=cast)
                late_w.extend(cast_out)
                c_rnn = u.shape[1]
                if name == "p":
                    conv0 = jnp.zeros((b, CONV_W - 1, c_rnn), F32)
                    h0 = jnp.zeros((b, c_rnn), F32)
                    tt = RGLRU_TT_PROMPT
                else:
                    conv0, h0, tt = state_conv[l], state_h[l], t
                hg, nconv, nh = _rglru(gate.reshape(b, t, c_rnn), u.reshape(b, t, c_rnn), conv0, h0,
                                       a_conv_w[l], a_conv_b[l], a_gate_w_b[l], a_gate_b[l], a_lambda[l], tt=tt)
                conv_out[name].append(nconv)
                h_out[name].append(nh.reshape(b, c_rnn))
                x = _out_proj(hg.reshape(b * t, c_rnn), a_w_out_b[l], x, tm=OUT_TM)
            else:
                jb = l - n_a
                q, g = _proj2(x, b_norm[jb], late_w[1 + jb], tm=PROJ_TM, a_epilogue="attn_scale",
                              b_epilogue="sigmoid", a_dtype=BF16)
                k, v, f_pad = kv[name][:3]
                if name == "p":
                    k_heads, v_heads = kv[name][3:]
                    c = _cumsum(f_pad.reshape(b, t, V7X_LANES), rows=CUMSUM_ROWS_PROMPT)
                    og = _attn_prompt(q.reshape(b, t, d), k_heads, v_heads, g.reshape(b, t, d), c,
                                      tile=ATTN_TILE)
                else:
                    cache_f_pad = jnp.pad(cache_logf, ((0, 0), (0, 0), (0, V7X_LANES - n_heads)))
                    f_all = jnp.concatenate([cache_f_pad, f_pad.reshape(b, t, V7X_LANES)], axis=1)
                    c = _cumsum(f_all, rows=CUMSUM_ROWS_SAMPLE)
                    og = _attn_sample(q.reshape(b, t, d), cache_k, cache_v, k.reshape(b, t, d),
                                      v.reshape(b, t, d), g.reshape(b, t, d), c)
                x = _out_proj(og.reshape(b * t, d), late_w[1 + n_b + jb], x, tm=OUT_TM)
            x = ffn(x, l, 1, name, apply_final=(l == depth - 1))
            if l == n_a - 1:
                kv[name] = _kv_proj(x, kv_norm, late_w[0], wf_pad, bf_pad, tm=PROJ_TM, head_major=(name == "p"))
            streams[name] = x

    def finish(name):
        b, t = batch[name]
        k, v, f_pad = kv[name][:3]
        return (streams[name].reshape(b, t, d), jnp.stack(conv_out[name]), jnp.stack(h_out[name]),
                k.reshape(b, t, n_heads, HEAD_DIM), v.reshape(b, t, n_heads, HEAD_DIM),
                f_pad[:, :n_heads].reshape(b, t, n_heads))

    yp, convp, hp, kp, vp, fp = finish("p")
    ys, convs, hs, ks, vs, fs = finish("s")
    return (yp, ys, convp, hp, kp, vp, fp, convs, hs, ks, vs, fs)
```
